```python
import math
import jax, jax.numpy as jnp
from jax import lax
import numpy as np

D_MODEL = 1024
BATCH = 1
SEQ = 16384
DEPTH = 2
DEC_BATCH = 32
DEC_SEQ = 8
PAST_LEN = 16384
PAGE_SIZE = 128

N_HEADS = 8
N_KV_HEADS = 4
GROUP = N_HEADS // N_KV_HEADS
HEAD_DIM = 64
V_DIM = 2 * HEAD_DIM
LRU_WIDTH = D_MODEL
LRU_BLOCKS = 16
LRU_BLOCK = LRU_WIDTH // LRU_BLOCKS
CONV_WIDTH = 4
LRU_C = 8.0
D_FF = 2816
N_BUCKETS = 32
MAX_DISTANCE = 128
Q_BLOCK = 128
NORM_EPS = 1e-6
N_MOD = 9
Q_COLS = N_HEADS * 2 * HEAD_DIM
K_COLS = N_KV_HEADS * 2 * HEAD_DIM
V_COLS = N_KV_HEADS * V_DIM
IN_COLS = Q_COLS + K_COLS + V_COLS + 2 * LRU_WIDTH + 2 * D_MODEL

kernel_name = 'hybrid_diffattn_rglru_macaron_step'


def rms_norm(x, g):
    xf = x.astype(jnp.float32)
    y = xf * lax.rsqrt(jnp.mean(xf * xf, axis=-1, keepdims=True) + NORM_EPS)
    return (y * g.astype(jnp.float32)).astype(x.dtype)


def swiglu(h, w_gu, w_down):
    gate, up = jnp.split(h @ w_gu, 2, axis=-1)
    return (jax.nn.silu(gate) * up) @ w_down


def rel_bucket(rel):
    n = jnp.maximum(-rel, 0)
    max_exact = N_BUCKETS // 2
    nf = jnp.maximum(n, 1).astype(jnp.float32)
    large = max_exact + (jnp.log(nf / max_exact) / math.log(MAX_DISTANCE / max_exact)
                         * (N_BUCKETS - max_exact)).astype(jnp.int32)
    large = jnp.minimum(large, N_BUCKETS - 1)
    return jnp.where(n < max_exact, n, large)


def diff_attend(q, q_pos, segments, rel_bias, lam, sub_g, lam_init):
    B, Q = q.shape[:2]
    qg = q.reshape(B, Q, N_KV_HEADS, GROUP, 2, HEAD_DIM)
    logit_list = []
    for k, v, k_pos in segments:
        lg = jnp.einsum('bqhgmd,bkhmd->bhgmqk', qg, k).astype(jnp.float32) * (HEAD_DIM ** -0.5)
        rel = k_pos[None, :] - q_pos[:, None]
        bias = rel_bias[rel_bucket(rel)].astype(jnp.float32)
        bias = jnp.transpose(bias, (2, 0, 1)).reshape(N_KV_HEADS, GROUP, 1, Q, -1)
        logit_list.append(jnp.where(rel <= 0, lg + bias, -jnp.inf))
    probs = jax.nn.softmax(jnp.concatenate(logit_list, axis=-1), axis=-1)
    attn = probs[:, :, :, 0] - lam * probs[:, :, :, 1]
    out = None
    start = 0
    for k, v, k_pos in segments:
        n = k.shape[1]
        part = jnp.einsum('bhgqk,bkhd->bqhgd', attn[..., start:start + n].astype(v.dtype), v)
        out = part if out is None else out + part
        start += n
    out = rms_norm(out, sub_g) * (1.0 - lam_init)
    return out.reshape(B, Q, N_HEADS * V_DIM)


def prompt_attention(q, segments, rel_bias, lam, sub_g, lam_init):
    B, S = q.shape[:2]
    nb = S // Q_BLOCK
    qb = jnp.moveaxis(q.reshape(B, nb, Q_BLOCK, N_HEADS, 2, HEAD_DIM), 1, 0)

    def block(args):
        q_blk, i = args
        q_pos = i * Q_BLOCK + jnp.arange(Q_BLOCK, dtype=jnp.int32)
        return diff_attend(q_blk, q_pos, segments, rel_bias, lam, sub_g, lam_init)

    out = lax.map(block, (qb, jnp.arange(nb, dtype=jnp.int32)))
    return jnp.moveaxis(out, 0, 1).reshape(B, S, N_HEADS * V_DIM)


def causal_conv(x, buf, w, b):
    T = x.shape[1]
    xp = jnp.concatenate([buf.astype(x.dtype), x], axis=1)
    y = b + xp[:, 0:T] * w[0]
    for j in range(1, CONV_WIDTH):
        y = y + xp[:, j:j + T] * w[j]
    return y, xp[:, T:]


def block_diag(x, w, b):
    B, T, _ = x.shape
    y = jnp.einsum('btnd,nde->btne', x.reshape(B, T, LRU_BLOCKS, LRU_BLOCK), w)
    return y.reshape(B, T, LRU_WIDTH) + b


def rg_lru(x, h0, wa, ba, wx, bx, lam):
    f32 = jnp.float32
    r = jax.nn.sigmoid(block_diag(x, wa, ba).astype(f32))
    i = jax.nn.sigmoid(block_diag(x, wx, bx).astype(f32))
    log_a = -LRU_C * r * jax.nn.softplus(-lam.astype(f32))
    a = jnp.exp(log_a)
    u = jnp.sqrt(-jnp.expm1(2.0 * log_a)) * i * x.astype(f32)
    u = u.at[:, 0].add(a[:, 0] * h0.astype(f32))

    def combine(e1, e2):
        a1, b1 = e1
        a2, b2 = e2
        return a1 * a2, a2 * b1 + b2

    _, h = lax.associative_scan(combine, (a, u), axis=1)
    return h.astype(x.dtype), h[:, -1].astype(x.dtype)


def run_trunk(x, c, p, cache_k=None, cache_v=None, page_table=None, state_h=None, state_conv=None):
    prompt = cache_k is None
    B, T, _ = x.shape
    past = 0 if prompt else page_table.shape[1] * cache_k.shape[2]
    q_pos = past + jnp.arange(T, dtype=jnp.int32)
    c_act = jax.nn.silu(c)
    sizes = [Q_COLS, K_COLS, V_COLS, LRU_WIDTH, LRU_WIDTH, 2 * D_MODEL]
    cuts = [int(s) for s in np.cumsum(sizes)[:-1]]
    ks, vs, hs, convs = [], [], [], []
    for l in range(DEPTH):
        mods = jnp.split(c_act @ p['ada_w'][l] + p['ada_b'][l], N_MOD, axis=-1)
        sh1, sc1, g1, sh2, sc2, g2, sh3, sc3, g3 = [m[:, None, :] for m in mods]
        h = rms_norm(x, p['ln_ffn1'][l]) * (1 + sc1) + sh1
        x = x + 0.5 * g1 * swiglu(h, p['ffn1_gu'][l], p['ffn1_down'][l])
        h = rms_norm(x, p['ln_mix'][l]) * (1 + sc2) + sh2
        q, k, v, xr, yr, gl = jnp.split(h @ p['w_in'][l], cuts, axis=-1)
        q = q.reshape(B, T, N_HEADS, 2, HEAD_DIM)
        k = k.reshape(B, T, N_KV_HEADS, 2 * HEAD_DIM)
        v = v.reshape(B, T, N_KV_HEADS, V_DIM)
        lam_init = 0.8 - 0.6 * math.exp(-0.3 * l)
        f32 = jnp.float32
        lam = (jnp.exp(jnp.sum(p['lambda_q1'][l].astype(f32) * p['lambda_k1'][l].astype(f32)))
               - jnp.exp(jnp.sum(p['lambda_q2'][l].astype(f32) * p['lambda_k2'][l].astype(f32)))
               + lam_init)
        new_seg = (k.reshape(B, T, N_KV_HEADS, 2, HEAD_DIM), v, q_pos)
        if prompt:
            attn = prompt_attention(q, [new_seg], p['rel_bias'], lam, p['attn_sub_g'][l], lam_init)
            buf = jnp.zeros((B, CONV_WIDTH - 1, LRU_WIDTH), x.dtype)
            h0 = jnp.zeros((B, LRU_WIDTH), x.dtype)
        else:
            pk = cache_k[l, page_table].reshape(B, past, N_KV_HEADS, 2, HEAD_DIM)
            pv = cache_v[l, page_table].reshape(B, past, N_KV_HEADS, V_DIM)
            segs = [(pk, pv, jnp.arange(past, dtype=jnp.int32)), new_seg]
            attn = diff_attend(q, q_pos, segs, p['rel_bias'], lam, p['attn_sub_g'][l], lam_init)
            buf = state_conv[l]
            h0 = state_h[l]
        xc, buf_new = causal_conv(xr, buf, p['conv_w'][l], p['conv_b'][l])
        hr, h_last = rg_lru(xc, h0, p['rg_wa'][l], p['rg_ba'][l], p['rg_wx'][l], p['rg_bx'][l],
                            p['rg_lambda'][l])
        lru_out = hr * jax.nn.gelu(yr)
        ga, gr = jnp.split(jax.nn.sigmoid(gl + p['b_gate'][l]), 2, axis=-1)
        merged = ga * (attn @ p['w_pa'][l]) + gr * (lru_out @ p['w_pr'][l])
        x = x + g2 * (merged @ p['w_o'][l])
        h = rms_norm(x, p['ln_ffn2'][l]) * (1 + sc3) + sh3
        x = x + 0.5 * g3 * swiglu(h, p['ffn2_gu'][l], p['ffn2_down'][l])
        ks.append(k)
        vs.append(v)
        hs.append(h_last)
        convs.append(buf_new)
    y = rms_norm(x, p['ln_final'])
    return y, jnp.stack(ks), jnp.stack(vs), jnp.stack(hs), jnp.stack(convs)


def setup_inputs(seed: int = 0) -> dict:
    key = jax.random.key(seed)
    keys = iter(jax.random.split(key, 48))
    f32 = jnp.float32

    def normal(shape, scale):
        return jax.random.normal(next(keys), shape, f32) * scale

    n_pages = PAST_LEN // PAGE_SIZE
    n_pool = (DEC_BATCH * n_pages * 5) // 4
    page_table = jax.random.permutation(next(keys), n_pool)[:DEC_BATCH * n_pages]
    page_table = page_table.reshape(DEC_BATCH, n_pages).astype(jnp.int32)
    a_base = jax.random.uniform(next(keys), (DEPTH, LRU_WIDTH), f32, 0.9, 0.999)
    s = a_base ** (1.0 / LRU_C)
    rg_lambda = jnp.log(s) - jnp.log1p(-s)
    D = D_MODEL
    return {
        'x_prompt': normal((BATCH, SEQ, D), 1.0),
        'x_sample': normal((DEC_BATCH, DEC_SEQ, D), 1.0),
        'cache_k': normal((DEPTH, n_pool, PAGE_SIZE, N_KV_HEADS, 2 * HEAD_DIM), 1.0),
        'cache_v': normal((DEPTH, n_pool, PAGE_SIZE, N_KV_HEADS, V_DIM), 1.0),
        'state_h': normal((DEPTH, DEC_BATCH, LRU_WIDTH), 0.5),
        'state_conv': normal((DEPTH, DEC_BATCH, CONV_WIDTH - 1, LRU_WIDTH), 1.0),
        'page_table': page_table,
        'c_prompt': normal((BATCH, D), 1.0),
        'c_sample': normal((DEC_BATCH, D), 1.0),
        'rel_bias': normal((N_BUCKETS, N_HEADS), 0.5),
        'ada_w': normal((DEPTH, D, N_MOD * D), D ** -0.5),
        'ada_b': normal((DEPTH, N_MOD * D), 0.02),
        'ln_ffn1': 1.0 + normal((DEPTH, D), 0.02),
        'ffn1_gu': normal((DEPTH, D, 2 * D_FF), D ** -0.5),
        'ffn1_down': normal((DEPTH, D_FF, D), D_FF ** -0.5),
        'ln_mix': 1.0 + normal((DEPTH, D), 0.02),
        'w_in': normal((DEPTH, D, IN_COLS), D ** -0.5),
        'b_gate': normal((DEPTH, 2 * D), 0.02),
        'lambda_q1': normal((DEPTH, HEAD_DIM), 0.1),
        'lambda_k1': normal((DEPTH, HEAD_DIM), 0.1),
        'lambda_q2': normal((DEPTH, HEAD_DIM), 0.1),
        'lambda_k2': normal((DEPTH, HEAD_DIM), 0.1),
        'attn_sub_g': 1.0 + normal((DEPTH, V_DIM), 0.02),
        'conv_w': normal((DEPTH, CONV_WIDTH, LRU_WIDTH), CONV_WIDTH ** -0.5),
        'conv_b': normal((DEPTH, LRU_WIDTH), 0.01),
        'rg_wa': normal((DEPTH, LRU_BLOCKS, LRU_BLOCK, LRU_BLOCK), LRU_BLOCK ** -0.5),
        'rg_ba': normal((DEPTH, LRU_WIDTH), 0.02),
        'rg_wx': normal((DEPTH, LRU_BLOCKS, LRU_BLOCK, LRU_BLOCK), LRU_BLOCK ** -0.5),
        'rg_bx': normal((DEPTH, LRU_WIDTH), 0.02),
        'rg_lambda': rg_lambda,
        'w_pa': normal((DEPTH, N_HEADS * V_DIM, D), (N_HEADS * V_DIM) ** -0.5),
        'w_pr': normal((DEPTH, LRU_WIDTH, D), LRU_WIDTH ** -0.5),
        'w_o': normal((DEPTH, D, D), D ** -0.5),
        'ln_ffn2': 1.0 + normal((DEPTH, D), 0.02),
        'ffn2_gu': normal((DEPTH, D, 2 * D_FF), D ** -0.5),
        'ffn2_down': normal((DEPTH, D_FF, D), D_FF ** -0.5),
        'ln_final': 1.0 + normal((D,), 0.02),
    }


def reference(x_prompt, x_sample, cache_k, cache_v, state_h, state_conv, page_table, c_prompt, c_sample,
              rel_bias, ada_w, ada_b, ln_ffn1, ffn1_gu, ffn1_down, ln_mix, w_in, b_gate,
              lambda_q1, lambda_k1, lambda_q2, lambda_k2, attn_sub_g, conv_w, conv_b,
              rg_wa, rg_ba, rg_wx, rg_bx, rg_lambda, w_pa, w_pr, w_o, ln_ffn2, ffn2_gu, ffn2_down,
              ln_final):
    p = dict(rel_bias=rel_bias, ada_w=ada_w, ada_b=ada_b, ln_ffn1=ln_ffn1, ffn1_gu=ffn1_gu,
             ffn1_down=ffn1_down, ln_mix=ln_mix, w_in=w_in, b_gate=b_gate, lambda_q1=lambda_q1,
             lambda_k1=lambda_k1, lambda_q2=lambda_q2, lambda_k2=lambda_k2, attn_sub_g=attn_sub_g,
             conv_w=conv_w, conv_b=conv_b, rg_wa=rg_wa, rg_ba=rg_ba, rg_wx=rg_wx, rg_bx=rg_bx,
             rg_lambda=rg_lambda, w_pa=w_pa, w_pr=w_pr, w_o=w_o, ln_ffn2=ln_ffn2, ffn2_gu=ffn2_gu,
             ffn2_down=ffn2_down, ln_final=ln_final)
    y_prompt, k_prompt, v_prompt, h_prompt, conv_prompt = run_trunk(x_prompt, c_prompt, p)
    y_sample, k_sample, v_sample, h_sample, conv_sample = run_trunk(
        x_sample, c_sample, p, cache_k, cache_v, page_table, state_h, state_conv)
    return (y_prompt, y_sample, k_prompt, v_prompt, h_prompt, conv_prompt,
            k_sample, v_sample, h_sample, conv_sample)
```

```python
import functools
import math

import jax
import jax.numpy as jnp
import numpy as np
from jax import lax
from jax.experimental import pallas as pl
from jax.experimental.pallas import tpu as pltpu

F32 = jnp.float32
BF16 = jnp.bfloat16

N_HEADS = 8
N_KV_HEADS = 4
GROUP = N_HEADS // N_KV_HEADS
HEAD_DIM = 64
V_DIM = 2 * HEAD_DIM
LRU_BLOCKS = 16
CONV_WIDTH = 4
LRU_C = 8.0
N_BUCKETS = 32
MAX_DISTANCE = 128
NORM_EPS = 1e-6
N_MOD = 9

V7X_LANES = 128
V7X_SUBLANES = 8
V7X_MXU_DIM = 256
V7X_VMEM_BYTES = 64 * 1024 * 1024
MIB = 1024 * 1024


def _params(semantics, vmem_mib):
    assert vmem_mib * MIB < V7X_VMEM_BYTES
    return pltpu.CompilerParams(dimension_semantics=semantics, vmem_limit_bytes=vmem_mib * MIB)


def _resident(shape, index_map):
    return pl.BlockSpec(shape, index_map, pipeline_mode=pl.Buffered(1))


def _sigmoid(x):
    return 1.0 / (1.0 + jnp.exp(-x))


def _modulated_norm(x, ln, sc, sh):
    y = x * lax.rsqrt(jnp.mean(x * x, axis=-1, keepdims=True) + NORM_EPS) * ln
    return y * (1.0 + sc) + sh


def _mod_spec(mod, tm):
    if mod.shape[0] == 1:
        return pl.BlockSpec((1, mod.shape[1]), lambda i: (0, 0))
    return pl.BlockSpec((tm, mod.shape[1]), lambda i: (i, 0))


def _ada_kernel(c_ref, w_ref, b_ref, o_ref):
    c = c_ref[...]
    act = c * _sigmoid(c)
    o_ref[...] = jnp.dot(act, w_ref[...], preferred_element_type=F32) + b_ref[...]


def _ada_mods(c_all, ada_w, ada_b):
    depth, d, n = ada_w.shape
    mp = c_all.shape[0]
    tn = n // 8
    assert tn % V7X_LANES == 0 and mp % V7X_SUBLANES == 0
    return pl.pallas_call(
        _ada_kernel,
        grid=(depth, n // tn),
        in_specs=[
            pl.BlockSpec((mp, d), lambda l, j: (0, 0)),
            pl.BlockSpec((None, d, tn), lambda l, j: (l, 0, j)),
            pl.BlockSpec((None, 1, tn), lambda l, j: (l, 0, j)),
        ],
        out_specs=pl.BlockSpec((None, mp, tn), lambda l, j: (l, 0, j)),
        out_shape=jax.ShapeDtypeStruct((depth, mp, n), F32),
        compiler_params=_params(("arbitrary", "arbitrary"), 24),
    )(c_all, ada_w, ada_b.reshape(depth, 1, n))


def _ffn_kernel(x_ref, sh_ref, sc_ref, g_ref, ln_ref, lnf_ref, wgu_ref, wd_ref, o_ref, *, d_ff, tf, final_norm):
    x = x_ref[...]
    h = _modulated_norm(x, ln_ref[...], sc_ref[...], sh_ref[...]).astype(BF16)
    acc = jnp.zeros(x.shape, F32)
    for c in range(d_ff // tf):
        gate = jnp.dot(h, wgu_ref[:, c * tf:(c + 1) * tf], preferred_element_type=F32)
        up = jnp.dot(h, wgu_ref[:, d_ff + c * tf:d_ff + (c + 1) * tf], preferred_element_type=F32)
        act = (gate * _sigmoid(gate) * up).astype(BF16)
        acc = acc + jnp.dot(act, wd_ref[c * tf:(c + 1) * tf, :], preferred_element_type=F32)
    y = x + 0.5 * g_ref[...] * acc
    if final_norm:
        y = y * lax.rsqrt(jnp.mean(y * y, axis=-1, keepdims=True) + NORM_EPS) * lnf_ref[...]
    o_ref[...] = y


def _ffn(x, sh, sc, g, ln, lnf, w_gu, w_down, *, final_norm):
    m, d = x.shape
    d_ff = w_down.shape[0]
    tm = min(512, m)
    tf = V7X_MXU_DIM
    assert m % tm == 0 and d_ff % tf == 0
    row = pl.BlockSpec((tm, d), lambda i: (i, 0))
    vec = pl.BlockSpec((1, d), lambda i: (0, 0))
    return pl.pallas_call(
        functools.partial(_ffn_kernel, d_ff=d_ff, tf=tf, final_norm=final_norm),
        grid=(m // tm,),
        in_specs=[row, _mod_spec(sh, tm), _mod_spec(sc, tm), _mod_spec(g, tm), vec, vec,
                  _resident(w_gu.shape, lambda i: (0, 0)), _resident(w_down.shape, lambda i: (0, 0))],
        out_specs=row,
        out_shape=jax.ShapeDtypeStruct((m, d), F32),
        compiler_params=_params(("arbitrary",), 48),
    )(x, sh, sc, g, ln, lnf, w_gu, w_down)


def _mixin_kernel(x_ref, sh_ref, sc_ref, ln_ref, w_ref, q_ref, k_ref, v_ref, xr_ref, yr_ref, gl_ref):
    h = _modulated_norm(x_ref[...], ln_ref[...], sc_ref[...], sh_ref[...]).astype(BF16)
    col = 0
    for ref, scale in ((q_ref, HEAD_DIM ** -0.5), (k_ref, None), (v_ref, None), (xr_ref, None), (yr_ref, None),
                       (gl_ref, None)):
        width = ref.shape[1]
        y = jnp.dot(h, w_ref[:, col:col + width], preferred_element_type=F32)
        if scale is not None:
            y = y * scale
        ref[...] = y.astype(ref.dtype)
        col += width


def _mixin(x, sh, sc, ln, w_in):
    m, d = x.shape
    q_cols, kv_cols = N_HEADS * 2 * HEAD_DIM, N_KV_HEADS * 2 * HEAD_DIM
    tm = min(256, m)
    assert m % tm == 0 and w_in.shape[1] == q_cols + 2 * kv_cols + 4 * d
    widths = (q_cols, kv_cols, kv_cols, d, d, 2 * d)
    dtypes = (BF16, F32, F32, F32, F32, F32)
    return pl.pallas_call(
        _mixin_kernel,
        grid=(m // tm,),
        in_specs=[pl.BlockSpec((tm, d), lambda i: (i, 0)), _mod_spec(sh, tm), _mod_spec(sc, tm),
                  pl.BlockSpec((1, d), lambda i: (0, 0)), _resident(w_in.shape, lambda i: (0, 0))],
        out_specs=[pl.BlockSpec((tm, w), lambda i: (i, 0)) for w in widths],
        out_shape=[jax.ShapeDtypeStruct((m, w), dt) for w, dt in zip(widths, dtypes)],
        compiler_params=_params(("arbitrary",), 48),
    )(x, sh, sc, ln, w_in)


def _rel_bucket(n):
    max_exact = N_BUCKETS // 2
    nf = jnp.maximum(n, 1).astype(F32)
    large = max_exact + (jnp.log(nf / max_exact) / math.log(MAX_DISTANCE / max_exact)
                         * (N_BUCKETS - max_exact)).astype(jnp.int32)
    large = jnp.minimum(large, N_BUCKETS - 1)
    return jnp.where(n < max_exact, n, large)


def _bias_of_distance(rel_bias, n):
    far = rel_bias[N_BUCKETS - 1].astype(F32)
    b = jnp.moveaxis(rel_bias[_rel_bucket(jnp.maximum(n, 0))].astype(F32) - far, -1, 0)
    return jnp.where(n >= 0, b, -jnp.inf)


def _lambda(lamp_ref, lam_init):
    lp = lamp_ref[...]
    return jnp.exp(jnp.sum(lp[0:1] * lp[1:2])) - jnp.exp(jnp.sum(lp[2:3] * lp[3:4])) + lam_init


def _head_norm(o, subg, lam_init):
    return o * lax.rsqrt(jnp.mean(o * o, axis=-1, keepdims=True) + NORM_EPS) * subg * (1.0 - lam_init)


def _online_softmax_step(s, v_blocks, m_ref, l_ref, acc_ref):
    m_old = m_ref[...]
    m_new = jnp.maximum(m_old, jnp.max(s, axis=-1, keepdims=True))
    alpha = jnp.exp(m_old - m_new)
    p = jnp.exp(s - m_new)
    l_ref[...] = alpha * l_ref[...] + jnp.sum(p, axis=-1, keepdims=True)
    pb = p.astype(BF16)
    nk = pb.shape[1] // len(v_blocks)
    pv = None
    for i, vb in enumerate(v_blocks):
        part = jnp.dot(pb[:, i * nk:(i + 1) * nk], vb, preferred_element_type=F32)
        pv = part if pv is None else pv + part
    acc_ref[...] = alpha * acc_ref[...] + pv
    m_ref[...] = m_new


M_INIT = -1e30


def _prompt_attn_kernel(q_ref, kt_ref, v_ref, bias_ref, lamp_ref, subg_ref, o_ref, m_scr, l_scr, acc_scr,
                        *, tq, lam_init):
    qi = pl.program_id(1)
    q = q_ref[...]
    qs = jnp.concatenate([q[:, :V_DIM], q[:, V_DIM:]], axis=0)
    lane = lax.broadcasted_iota(jnp.int32, qs.shape, 1)
    zero = jnp.zeros_like(qs)
    q_maps = (jnp.where(lane < HEAD_DIM, qs, zero), jnp.where(lane >= HEAD_DIM, qs, zero))

    m_scr[...] = jnp.full(m_scr.shape, M_INIT, F32)
    l_scr[...] = jnp.zeros(l_scr.shape, F32)
    acc_scr[...] = jnp.zeros(acc_scr.shape, F32)

    def block(j, bias):
        kt = kt_ref[j]
        vb = v_ref[pl.ds(pl.multiple_of(j * tq, tq), tq), :]
        for mp in range(2):
            s = jnp.dot(q_maps[mp], kt, preferred_element_type=F32)
            if bias is not None:
                s = s + bias
            _online_softmax_step(s, [vb], m_scr.at[mp], l_scr.at[mp], acc_scr.at[mp])

    def far_block(j, carry):
        block(j, None)
        return carry

    lax.fori_loop(0, jnp.maximum(qi - 1, 0), far_block, 0)

    @pl.when(qi > 0)
    def _():
        block(qi - 1, bias_ref[:, :tq])

    block(qi, bias_ref[:, tq:])

    lam = _lambda(lamp_ref, lam_init)
    o = acc_scr[0] / l_scr[0] - lam * (acc_scr[1] / l_scr[1])
    o = _head_norm(o, subg_ref[...], lam_init)
    o_ref[:, :V_DIM] = o[:tq].astype(o_ref.dtype)
    o_ref[:, V_DIM:] = o[tq:].astype(o_ref.dtype)


def _prompt_attention(q, k, v, rel_bias, lamp, subg, lam_init):
    s_len = q.shape[0]
    tq = min(256, s_len)
    nq = s_len // tq
    assert s_len % tq == 0 and tq >= MAX_DISTANCE
    kt = k.astype(BF16).reshape(nq, tq, N_KV_HEADS, V_DIM).transpose(2, 0, 3, 1)
    vb = v.astype(BF16)
    r = jnp.arange(tq, dtype=jnp.int32)[:, None]
    c = jnp.arange(2 * tq, dtype=jnp.int32)[None, :]
    bias = _bias_of_distance(rel_bias, tq + r - c).reshape(N_KV_HEADS, GROUP * tq, 2 * tq)
    rows = GROUP * tq
    return pl.pallas_call(
        functools.partial(_prompt_attn_kernel, tq=tq, lam_init=lam_init),
        grid=(N_KV_HEADS, nq),
        in_specs=[
            pl.BlockSpec((tq, GROUP * V_DIM), lambda h, i: (i, h)),
            pl.BlockSpec((None, nq, V_DIM, tq), lambda h, i: (h, 0, 0, 0)),
            pl.BlockSpec((s_len, V_DIM), lambda h, i: (0, h)),
            pl.BlockSpec((None, rows, 2 * tq), lambda h, i: (h, 0, 0)),
            pl.BlockSpec(lamp.shape, lambda h, i: (0, 0)),
            pl.BlockSpec(subg.shape, lambda h, i: (0, 0)),
        ],
        out_specs=pl.BlockSpec((tq, GROUP * V_DIM), lambda h, i: (i, h)),
        out_shape=jax.ShapeDtypeStruct((s_len, N_HEADS * V_DIM), BF16),
        scratch_shapes=[pltpu.VMEM((2, rows, 1), F32), pltpu.VMEM((2, rows, 1), F32),
                        pltpu.VMEM((2, rows, V_DIM), F32)],
        compiler_params=_params(("arbitrary", "arbitrary"), 48),
    )(q, kt, vb, bias, lamp, subg)


def _decode_attn_kernel(pt_ref, q_ref, bias_ref, knew_ref, vnew_ref, bnew_ref, lamp_ref, subg_ref, *rest,
                        pages, lam_init):
    del pt_ref
    k_refs, v_refs = rest[:pages], rest[pages:2 * pages]
    o_ref, m_scr, l_scr, acc_scr = rest[2 * pages:]
    step = pl.program_id(1)

    @pl.when(step == 0)
    def _():
        m_scr[...] = jnp.full(m_scr.shape, M_INIT, F32)
        l_scr[...] = jnp.zeros(l_scr.shape, F32)
        acc_scr[...] = jnp.zeros(acc_scr.shape, F32)

    q = q_ref[...]
    nt = (((1,), (1,)), ((), ()))

    def logits(k):
        return lax.dot_general(q, k.astype(BF16), nt, preferred_element_type=F32)

    s = jnp.concatenate([logits(k_refs[i][...]) for i in range(pages)], axis=1) + bias_ref[...]
    _online_softmax_step(s, [v_refs[i][...].astype(BF16) for i in range(pages)], m_scr, l_scr, acc_scr)

    @pl.when(step == pl.num_programs(1) - 1)
    def _():
        s_new = logits(knew_ref[...]) + bnew_ref[...]
        _online_softmax_step(s_new, [vnew_ref[...].astype(BF16)], m_scr, l_scr, acc_scr)
        lam = _lambda(lamp_ref, lam_init)
        rows = q.shape[0] // N_KV_HEADS
        for h in range(N_KV_HEADS):
            blk = acc_scr[h * rows:(h + 1) * rows, h * V_DIM:(h + 1) * V_DIM] / l_scr[h * rows:(h + 1) * rows, :]
            o = blk[:rows // 2] - lam * blk[rows // 2:]
            o_ref[h * (rows // 2):(h + 1) * (rows // 2), :] = _head_norm(o, subg_ref[...], lam_init).astype(o_ref.dtype)


def _decode_attention(q, k_new, v_new, cache_k, cache_v, layer, page_table, rel_bias, lamp, subg, lam_init):
    batch, n_pages = page_table.shape
    depth, n_pool, page, kvh, kdim = cache_k.shape
    t_len = q.shape[0] // batch
    cols = N_KV_HEADS * 2 * GROUP * t_len
    width = kvh * kdim
    pages = 8
    assert n_pages % pages == 0 and page >= MAX_DISTANCE and t_len <= page and kvh == N_KV_HEADS
    n_steps = n_pages // pages
    ck = cache_k.reshape(depth, n_pool, page, width)
    cv = cache_v.reshape(depth, n_pool, page, width)

    q6 = q.reshape(batch, t_len, N_KV_HEADS, GROUP, 2, HEAD_DIM).transpose(0, 2, 4, 3, 1, 5)
    eye_h = jnp.eye(N_KV_HEADS, dtype=q.dtype)[None, :, None, None, None, :, None, None]
    eye_m = jnp.eye(2, dtype=q.dtype)[None, None, :, None, None, None, :, None]
    q_all = (q6[:, :, :, :, :, None, None, :] * eye_h * eye_m).reshape(batch, cols, width)

    row_h = (jnp.arange(N_KV_HEADS)[:, None, None, None] * GROUP + jnp.arange(GROUP)[None, None, :, None]
             + jnp.zeros((1, 2, 1, t_len), jnp.int32)).reshape(cols)
    row_t = (jnp.zeros((N_KV_HEADS, 2, GROUP, 1), jnp.int32) + jnp.arange(t_len, dtype=jnp.int32)).reshape(cols)
    key = jnp.arange(page, dtype=jnp.int32)[None, :]
    sel = jax.nn.one_hot(row_h, N_HEADS, dtype=F32)
    last = jnp.einsum('hck,ch->ck', _bias_of_distance(rel_bias, page + row_t[:, None] - key), sel)
    bias_steps = jnp.zeros((2, cols, pages * page), F32).at[1, :, (pages - 1) * page:].set(last)
    n_new = jnp.where(key < t_len, row_t[:, None] - key, -1)
    b_all = _bias_of_distance(rel_bias, n_new)
    b_new = jnp.sum(jnp.where(sel.T[:, :, None] > 0, b_all, 0.0), axis=0)
    k_pad = jnp.pad(k_new.reshape(batch, t_len, width), ((0, 0), (0, page - t_len), (0, 0)))
    v_pad = jnp.pad(v_new.reshape(batch, t_len, width), ((0, 0), (0, page - t_len), (0, 0)))

    def page_spec(i):
        return pl.BlockSpec((None, None, page, width), lambda b, s, pt: (layer, pt[b, s * pages + i], 0, 0))

    grid_spec = pltpu.PrefetchScalarGridSpec(
        num_scalar_prefetch=1,
        grid=(batch, n_steps),
        in_specs=[
            pl.BlockSpec((None, cols, width), lambda b, s, pt: (b, 0, 0)),
            pl.BlockSpec((None, cols, pages * page), lambda b, s, pt: (jnp.where(s == n_steps - 1, 1, 0), 0, 0)),
            pl.BlockSpec((None, page, width), lambda b, s, pt: (b, 0, 0)),
            pl.BlockSpec((None, page, width), lambda b, s, pt: (b, 0, 0)),
            pl.BlockSpec((cols, page), lambda b, s, pt: (0, 0)),
            pl.BlockSpec(lamp.shape, lambda b, s, pt: (0, 0)),
            pl.BlockSpec(subg.shape, lambda b, s, pt: (0, 0)),
        ] + [page_spec(i) for i in range(pages)] * 2,
        out_specs=pl.BlockSpec((None, cols // 2, V_DIM), lambda b, s, pt: (b, 0, 0)),
        scratch_shapes=[pltpu.VMEM((cols, 1), F32), pltpu.VMEM((cols, 1), F32), pltpu.VMEM((cols, width), F32)],
    )
    out = pl.pallas_call(
        functools.partial(_decode_attn_kernel, pages=pages, lam_init=lam_init),
        grid_spec=grid_spec,
        out_shape=jax.ShapeDtypeStruct((batch, cols // 2, V_DIM), BF16),
        compiler_params=_params(("arbitrary", "arbitrary"), 40),
    )(page_table, q_all, bias_steps, k_pad, v_pad, b_new, lamp, subg, *([ck] * pages), *([cv] * pages))
    out = out.reshape(batch, N_KV_HEADS, GROUP, t_len, V_DIM).transpose(0, 3, 1, 2, 4)
    return out.reshape(batch * t_len, N_HEADS * V_DIM)


def _gelu_tanh(x):
    return 0.5 * x * (1.0 + jnp.tanh(math.sqrt(2.0 / math.pi) * (x + 0.044715 * (x * x * x))))


def _lru_kernel(xr_ref, yr_ref, h0_ref, buf0_ref, cw_ref, cb_ref, wa_ref, ba_ref, wx_ref, bx_ref, lam_ref,
                o_ref, hlast_ref, tail_ref, xp_scr, h_scr, *, tb):
    pad = V7X_SUBLANES
    tail = CONV_WIDTH - 1

    @pl.when(pl.program_id(1) == 0)
    def _():
        h_scr[...] = h0_ref[...]
        xp_scr[pad - tail:pad, :] = buf0_ref[...]

    x = xr_ref[...]
    xp_scr[pad:pad + tb, :] = x
    cw = cw_ref[...]
    xc = cb_ref[...] + xp_scr[pad - tail:pad - tail + tb, :] * cw[0:1]
    for j in range(1, CONV_WIDTH):
        xc = xc + xp_scr[pad - tail + j:pad - tail + j + tb, :] * cw[j:j + 1]
    new_tail = xp_scr[pad + tb - tail:pad + tb, :]
    tail_ref[...] = new_tail
    xp_scr[pad - tail:pad, :] = new_tail

    xcb = xc.astype(BF16)
    nblk, bw = wa_ref.shape[0], wa_ref.shape[1]

    def gate(w_ref, b_ref):
        parts = [jnp.dot(xcb[:, c * bw:(c + 1) * bw], w_ref[c], preferred_element_type=F32) for c in range(nblk)]
        return _sigmoid(jnp.concatenate(parts, axis=1) + b_ref[...])

    r = gate(wa_ref, ba_ref)
    i = gate(wx_ref, bx_ref)
    neg_lam = -lam_ref[...]
    softplus = jnp.maximum(neg_lam, 0.0) + jnp.log(1.0 + jnp.exp(-jnp.abs(neg_lam)))
    log_a = -LRU_C * r * softplus
    a = jnp.exp(log_a)
    u = jnp.sqrt(-jnp.tanh(log_a) * (a * a + 1.0)) * i * xc

    row = lax.broadcasted_iota(jnp.int32, a.shape, 0)
    big_a, big_b = a, u
    shift = 1
    while shift < tb:
        keep = row >= shift
        a_prev = jnp.where(keep, pltpu.roll(big_a, shift, 0), 1.0)
        b_prev = jnp.where(keep, pltpu.roll(big_b, shift, 0), 0.0)
        big_b = big_a * b_prev + big_b
        big_a = big_a * a_prev
        shift *= 2
    h = big_a * h_scr[...] + big_b
    h_scr[...] = h[tb - 1:tb]
    hlast_ref[...] = h[tb - 1:tb]
    o_ref[...] = (h * _gelu_tanh(yr_ref[...])).astype(o_ref.dtype)


def _lru(xr, yr, h0, buf0, conv_w, conv_b, wa, ba, wx, bx, lam, *, batch):
    m, w = xr.shape
    t_len = m // batch
    tb = min(256, t_len)
    nt = t_len // tb
    assert t_len % tb == 0 and tb % V7X_SUBLANES == 0
    tail = CONV_WIDTH - 1
    xr3, yr3 = xr.reshape(batch, t_len, w), yr.reshape(batch, t_len, w)
    rows = pl.BlockSpec((None, tb, w), lambda b, t: (b, t, 0))
    vec = pl.BlockSpec((1, w), lambda b, t: (0, 0))
    wspec = pl.BlockSpec(wa.shape, lambda b, t: (0, 0, 0))
    out, h_last, conv_tail = pl.pallas_call(
        functools.partial(_lru_kernel, tb=tb),
        grid=(batch, nt),
        in_specs=[rows, rows,
                  pl.BlockSpec((None, 1, w), lambda b, t: (b, 0, 0)),
                  pl.BlockSpec((None, tail, w), lambda b, t: (b, 0, 0)),
                  pl.BlockSpec((CONV_WIDTH, w), lambda b, t: (0, 0)), vec, wspec, vec, wspec, vec, vec],
        out_specs=[rows,
                   pl.BlockSpec((None, 1, w), lambda b, t: (b, 0, 0)),
                   pl.BlockSpec((None, tail, w), lambda b, t: (b, 0, 0))],
        out_shape=[jax.ShapeDtypeStruct((batch, t_len, w), BF16),
                   jax.ShapeDtypeStruct((batch, 1, w), F32),
                   jax.ShapeDtypeStruct((batch, tail, w), F32)],
        scratch_shapes=[pltpu.VMEM((V7X_SUBLANES + tb, w), F32), pltpu.VMEM((1, w), F32)],
        compiler_params=_params(("arbitrary", "arbitrary"), 48),
    )(xr3, yr3, h0.reshape(batch, 1, w), buf0, conv_w, conv_b, wa, ba, wx, bx, lam)
    return out.reshape(m, w), h_last.reshape(batch, w), conv_tail


def _block_diag(w, per):
    n, d, e = w.shape
    wg = w.reshape(n // per, per, d, e)
    eye = jnp.eye(per, dtype=w.dtype)
    return jnp.einsum('gpde,pq->gpdqe', wg, eye).reshape(n // per, per * d, per * e)


def _merge_kernel(x_ref, attn_ref, lru_ref, gla_ref, glr_ref, bga_ref, bgr_ref, g_ref, wpa_ref, wpr_ref, wo_ref,
                  o_ref):
    pa = jnp.dot(attn_ref[...], wpa_ref[...], preferred_element_type=F32)
    pr = jnp.dot(lru_ref[...], wpr_ref[...], preferred_element_type=F32)
    merged = _sigmoid(gla_ref[...] + bga_ref[...]) * pa + _sigmoid(glr_ref[...] + bgr_ref[...]) * pr
    o_ref[...] = x_ref[...] + g_ref[...] * jnp.dot(merged.astype(BF16), wo_ref[...], preferred_element_type=F32)


def _merge(x, attn, lru_out, gl, b_gate, g, w_pa, w_pr, w_o):
    m, d = x.shape
    tm = min(512, m)
    assert m % tm == 0
    row = pl.BlockSpec((tm, d), lambda i: (i, 0))
    wspec = _resident((d, d), lambda i: (0, 0))
    return pl.pallas_call(
        _merge_kernel,
        grid=(m // tm,),
        in_specs=[row, row, row, row, pl.BlockSpec((tm, d), lambda i: (i, 1)),
                  pl.BlockSpec((1, d), lambda i: (0, 0)), pl.BlockSpec((1, d), lambda i: (0, 1)),
                  _mod_spec(g, tm), wspec, wspec, wspec],
        out_specs=row,
        out_shape=jax.ShapeDtypeStruct((m, d), F32),
        compiler_params=_params(("arbitrary",), 48),
    )(x, attn, lru_out, gl, gl, b_gate, b_gate, g, w_pa, w_pr, w_o)


def _run_trunk(x, mods, p, *, batch, cache=None):
    depth = p['w_in'].shape[0]
    d = x.shape[1]
    ks, vs, hs, convs = [], [], [], []
    for l in range(depth):
        sh1, sc1, g1, sh2, sc2, g2, sh3, sc3, g3 = mods[l]
        lam_init = 0.8 - 0.6 * math.exp(-0.3 * l)
        lamp = p['lamp'][l]
        subg = p['attn_sub_g'][l][None, :]
        x = _ffn(x, sh1, sc1, g1, p['ln_ffn1'][l][None], p['ln_final'][None], p['ffn1_gu'][l], p['ffn1_down'][l],
                 final_norm=False)
        q, k, v, xr, yr, gl = _mixin(x, sh2, sc2, p['ln_mix'][l][None], p['w_in'][l])
        if cache is None:
            attn = _prompt_attention(q, k, v, p['rel_bias'], lamp, subg, lam_init)
            h0 = jnp.zeros((batch, d), F32)
            buf0 = jnp.zeros((batch, CONV_WIDTH - 1, d), F32)
        else:
            cache_k, cache_v, page_table, state_h, state_conv = cache
            attn = _decode_attention(q, k, v, cache_k, cache_v, l, page_table, p['rel_bias'], lamp, subg, lam_init)
            h0, buf0 = state_h[l], state_conv[l]
        lru_out, h_last, conv_tail = _lru(xr, yr, h0, buf0, p['conv_w'][l], p['conv_b'][l][None], p['rg_wa'][l],
                                          p['rg_ba'][l][None], p['rg_wx'][l], p['rg_bx'][l][None],
                                          p['rg_lambda'][l][None], batch=batch)
        x = _merge(x, attn, lru_out, gl, p['b_gate'][l][None], g2, p['w_pa'][l], p['w_pr'][l], p['w_o'][l])
        x = _ffn(x, sh3, sc3, g3, p['ln_ffn2'][l][None], p['ln_final'][None], p['ffn2_gu'][l], p['ffn2_down'][l],
                 final_norm=(l == depth - 1))
        t_len = x.shape[0] // batch
        ks.append(k.reshape(batch, t_len, N_KV_HEADS, 2 * HEAD_DIM))
        vs.append(v.reshape(batch, t_len, N_KV_HEADS, V_DIM))
        hs.append(h_last)
        convs.append(conv_tail)
    return x, jnp.stack(ks), jnp.stack(vs), jnp.stack(hs), jnp.stack(convs)


def kernel(x_prompt, x_sample, cache_k, cache_v, state_h, state_conv, page_table, c_prompt, c_sample, rel_bias, ada_w, ada_b, ln_ffn1, ffn1_gu, ffn1_down, ln_mix, w_in, b_gate, lambda_q1, lambda_k1, lambda_q2, lambda_k2, attn_sub_g, conv_w, conv_b, rg_wa, rg_ba, rg_wx, rg_bx, rg_lambda, w_pa, w_pr, w_o, ln_ffn2, ffn2_gu, ffn2_down, ln_final):
    b_p, s_len, d = x_prompt.shape
    b_d, t_dec, _ = x_sample.shape
    depth = w_in.shape[0]
    per = V7X_MXU_DIM // (rg_wa.shape[-1])
    p = dict(
        rel_bias=rel_bias, ln_ffn1=ln_ffn1, ln_mix=ln_mix, ln_ffn2=ln_ffn2, ln_final=ln_final, b_gate=b_gate,
        attn_sub_g=attn_sub_g, conv_w=conv_w, conv_b=conv_b, rg_ba=rg_ba, rg_bx=rg_bx, rg_lambda=rg_lambda,
        ffn1_gu=ffn1_gu.astype(BF16), ffn1_down=ffn1_down.astype(BF16), w_in=w_in.astype(BF16),
        w_pa=w_pa.astype(BF16), w_pr=w_pr.astype(BF16), w_o=w_o.astype(BF16),
        ffn2_gu=ffn2_gu.astype(BF16), ffn2_down=ffn2_down.astype(BF16),
        rg_wa=jnp.stack([_block_diag(rg_wa[l].astype(BF16), per) for l in range(depth)]),
        rg_wx=jnp.stack([_block_diag(rg_wx[l].astype(BF16), per) for l in range(depth)]),
        lamp=jnp.stack([lambda_q1, lambda_k1, lambda_q2, lambda_k2], axis=1).astype(F32),
    )

    n_seq = b_p + b_d
    pad = (-n_seq) % V7X_SUBLANES
    c_all = jnp.pad(jnp.concatenate([c_prompt, c_sample], axis=0), ((0, pad), (0, 0)))
    mods = _ada_mods(c_all, ada_w, ada_b)

    def split(rows, repeat):
        out = []
        for l in range(depth):
            parts = [rows[l][:, k * d:(k + 1) * d] for k in range(N_MOD)]
            out.append([jnp.repeat(m, repeat, axis=0) if repeat > 1 else m for m in parts])
        return out

    assert b_p == 1
    mods_p = split(mods[:, :b_p], 1)
    mods_d = split(mods[:, b_p:n_seq], t_dec)

    y_p, k_p, v_p, h_p, conv_p = _run_trunk(x_prompt.reshape(b_p * s_len, d), mods_p, p, batch=b_p)
    y_d, k_d, v_d, h_d, conv_d = _run_trunk(x_sample.reshape(b_d * t_dec, d), mods_d, p, batch=b_d,
                                            cache=(cache_k, cache_v, page_table, state_h, state_conv))
    return (y_p.reshape(b_p, s_len, d), y_d.reshape(b_d, t_dec, d), k_p, v_p, h_p, conv_p, k_d, v_d, h_d, conv_d)
```

```python
import functools
import math

import jax
import jax.numpy as jnp
import numpy as np
from jax import lax
from jax.experimental import pallas as pl
from jax.experimental.pallas import tpu as pltpu

F32 = jnp.float32
BF16 = jnp.bfloat16

N_HEADS = 8
N_KV_HEADS = 4
GROUP = N_HEADS // N_KV_HEADS
HEAD_DIM = 64
V_DIM = 2 * HEAD_DIM
LRU_BLOCKS = 16
CONV_WIDTH = 4
LRU_C = 8.0
N_BUCKETS = 32
MAX_DISTANCE = 128
NORM_EPS = 1e-6
N_MOD = 9

V7X_LANES = 128
V7X_SUBLANES = 8
V7X_MXU_DIM = 256
V7X_VMEM_BYTES = 64 * 1024 * 1024
MIB = 1024 * 1024


def _params(semantics, vmem_mib):
    assert vmem_mib * MIB < V7X_VMEM_BYTES
    return pltpu.CompilerParams(dimension_semantics=semantics, vmem_limit_bytes=vmem_mib * MIB)


def _resident(shape, index_map):
    return pl.BlockSpec(shape, index_map, pipeline_mode=pl.Buffered(1))


def _sigmoid(x):
    return 1.0 / (1.0 + jnp.exp(-x))


def _modulated_norm(x, ln, sc, sh):
    y = x * lax.rsqrt(jnp.mean(x * x, axis=-1, keepdims=True) + NORM_EPS) * ln
    return y * (1.0 + sc) + sh


def _mod_spec(mod, tm):
    if mod.shape[0] == 1:
        return pl.BlockSpec((1, mod.shape[1]), lambda i: (0, 0))
    return pl.BlockSpec((tm, mod.shape[1]), lambda i: (i, 0))


def _ada_kernel(c_ref, w_ref, b_ref, o_ref):
    c = c_ref[...]
    act = c * _sigmoid(c)
    o_ref[...] = jnp.dot(act, w_ref[...], preferred_element_type=F32) + b_ref[...]


def _ada_mods(c_all, ada_w, ada_b):
    depth, d, n = ada_w.shape
    mp = c_all.shape[0]
    tn = n // 8
    assert tn % V7X_LANES == 0 and mp % V7X_SUBLANES == 0
    return pl.pallas_call(
        _ada_kernel,
        grid=(depth, n // tn),
        in_specs=[
            pl.BlockSpec((mp, d), lambda l, j: (0, 0)),
            pl.BlockSpec((None, d, tn), lambda l, j: (l, 0, j)),
            pl.BlockSpec((None, 1, tn), lambda l, j: (l, 0, j)),
        ],
        out_specs=pl.BlockSpec((None, mp, tn), lambda l, j: (l, 0, j)),
        out_shape=jax.ShapeDtypeStruct((depth, mp, n), F32),
        compiler_params=_params(("arbitrary", "arbitrary"), 24),
    )(c_all, ada_w, ada_b.reshape(depth, 1, n))


def _ffn_kernel(x_ref, sh_ref, sc_ref, g_ref, ln_ref, lnf_ref, wgu_ref, wd_ref, o_ref, *, d_ff, tf, final_norm):
    x = x_ref[...]
    h = _modulated_norm(x, ln_ref[...], sc_ref[...], sh_ref[...]).astype(BF16)
    acc = jnp.zeros(x.shape, F32)
    for c in range(d_ff // tf):
        gate = jnp.dot(h, wgu_ref[:, c * tf:(c + 1) * tf], preferred_element_type=F32)
        up = jnp.dot(h, wgu_ref[:, d_ff + c * tf:d_ff + (c + 1) * tf], preferred_element_type=F32)
        act = (gate * _sigmoid(gate) * up).astype(BF16)
        acc = acc + jnp.dot(act, wd_ref[c * tf:(c + 1) * tf, :], preferred_element_type=F32)
    y = x + 0.5 * g_ref[...] * acc
    if final_norm:
        y = y * lax.rsqrt(jnp.mean(y * y, axis=-1, keepdims=True) + NORM_EPS) * lnf_ref[...]
    o_ref[...] = y


def _ffn(x, sh, sc, g, ln, lnf, w_gu, w_down, *, final_norm):
    m, d = x.shape
    d_ff = w_down.shape[0]
    tm = min(512, m)
    tf = V7X_MXU_DIM
    assert m % tm == 0 and d_ff % tf == 0
    row = pl.BlockSpec((tm, d), lambda i: (i, 0))
    vec = pl.BlockSpec((1, d), lambda i: (0, 0))
    return pl.pallas_call(
        functools.partial(_ffn_kernel, d_ff=d_ff, tf=tf, final_norm=final_norm),
        grid=(m // tm,),
        in_specs=[row, _mod_spec(sh, tm), _mod_spec(sc, tm), _mod_spec(g, tm), vec, vec,
                  _resident(w_gu.shape, lambda i: (0, 0)), _resident(w_down.shape, lambda i: (0, 0))],
        out_specs=row,
        out_shape=jax.ShapeDtypeStruct((m, d), F32),
        compiler_params=_params(("arbitrary",), 48),
    )(x, sh, sc, g, ln, lnf, w_gu, w_down)


def _mixin_kernel(x_ref, sh_ref, sc_ref, ln_ref, w_ref, q_ref, k_ref, v_ref, xr_ref, yr_ref, gl_ref):
    h = _modulated_norm(x_ref[...], ln_ref[...], sc_ref[...], sh_ref[...]).astype(BF16)
    col = 0
    for ref, scale in ((q_ref, HEAD_DIM ** -0.5), (k_ref, None), (v_ref, None), (xr_ref, None), (yr_ref, None),
                       (gl_ref, None)):
        width = ref.shape[1]
        y = jnp.dot(h, w_ref[:, col:col + width], preferred_element_type=F32)
        if scale is not None:
            y = y * scale
        ref[...] = y.astype(ref.dtype)
        col += width


def _mixin(x, sh, sc, ln, w_in):
    m, d = x.shape
    q_cols, kv_cols = N_HEADS * 2 * HEAD_DIM, N_KV_HEADS * 2 * HEAD_DIM
    tm = min(256, m)
    assert m % tm == 0 and w_in.shape[1] == q_cols + 2 * kv_cols + 4 * d
    widths = (q_cols, kv_cols, kv_cols, d, d, 2 * d)
    dtypes = (BF16, F32, F32, F32, F32, F32)
    return pl.pallas_call(
        _mixin_kernel,
        grid=(m // tm,),
        in_specs=[pl.BlockSpec((tm, d), lambda i: (i, 0)), _mod_spec(sh, tm), _mod_spec(sc, tm),
                  pl.BlockSpec((1, d), lambda i: (0, 0)), _resident(w_in.shape, lambda i: (0, 0))],
        out_specs=[pl.BlockSpec((tm, w), lambda i: (i, 0)) for w in widths],
        out_shape=[jax.ShapeDtypeStruct((m, w), dt) for w, dt in zip(widths, dtypes)],
        compiler_params=_params(("arbitrary",), 48),
    )(x, sh, sc, ln, w_in)


def _rel_bucket(n):
    max_exact = N_BUCKETS // 2
    nf = jnp.maximum(n, 1).astype(F32)
    large = max_exact + (jnp.log(nf / max_exact) / math.log(MAX_DISTANCE / max_exact)
                         * (N_BUCKETS - max_exact)).astype(jnp.int32)
    large = jnp.minimum(large, N_BUCKETS - 1)
    return jnp.where(n < max_exact, n, large)


def _bias_of_distance(rel_bias, n):
    far = rel_bias[N_BUCKETS - 1].astype(F32)
    b = jnp.moveaxis(rel_bias[_rel_bucket(jnp.maximum(n, 0))].astype(F32) - far, -1, 0)
    return jnp.where(n >= 0, b, -jnp.inf)


def _lambda(lamp_ref, lam_init):
    lp = lamp_ref[...]
    return jnp.exp(jnp.sum(lp[0:1] * lp[1:2])) - jnp.exp(jnp.sum(lp[2:3] * lp[3:4])) + lam_init


M_INIT = -1e30


ACC_ROWS = V_DIM + 16


def _col_max(s):
    rows = s.shape[0]
    while rows > V7X_SUBLANES:
        rows //= 2
        s = jnp.maximum(s[:rows], s[rows:])
    return jnp.max(s, axis=0, keepdims=True)


def _prompt_attn_kernel(qt_ref, k_ref, vt_ref, bias_ref, lamp_ref, subg_ref, o_ref,
                        qz_scr, sa_scr, sb_scr, m_scr, acc_scr, *, tq, tk, lam_init):
    qi = pl.program_id(1)
    ncomb = 2 * GROUP
    ratio = tq // tk
    row = lax.broadcasted_iota(jnp.int32, (V_DIM, tq), 0)
    for g in range(GROUP):
        qg = qt_ref[g * V_DIM:(g + 1) * V_DIM, :]
        qz_scr[:, (2 * g) * tq:(2 * g + 1) * tq] = jnp.where(row < HEAD_DIM, qg, jnp.zeros_like(qg))
        qz_scr[:, (2 * g + 1) * tq:(2 * g + 2) * tq] = jnp.where(row >= HEAD_DIM, qg, jnp.zeros_like(qg))
    m_scr[...] = jnp.full(m_scr.shape, M_INIT, F32)
    acc_scr[...] = jnp.zeros(acc_scr.shape, F32)
    s_bufs = (sa_scr, sb_scr)

    def logits(j, dst):
        kb = k_ref[pl.ds(pl.multiple_of(j * tk, tk), tk), :]
        dst[...] = jnp.dot(kb, qz_scr[...], preferred_element_type=F32)

    def update(j, src, bias_row):
        vt = vt_ref[j]
        for c in range(ncomb):
            s = src[:, c * tq:(c + 1) * tq]
            if bias_row is not None:
                g = c // 2
                s = s + bias_ref[bias_row:bias_row + tk, g * tq:(g + 1) * tq]
            m_old = m_scr[c]
            m_new = jnp.maximum(m_old, _col_max(s))
            alpha = jnp.exp(m_old - m_new)
            p = jnp.exp(s - m_new).astype(BF16)
            acc_scr[c] = alpha * acc_scr[c] + jnp.dot(vt, p, preferred_element_type=F32)
            m_scr[c] = m_new

    def step(j, with_next, bias_row):
        for parity in range(2):
            @pl.when(j % 2 == parity)
            def _():
                if with_next:
                    logits(j + 1, s_bufs[1 - parity])
                update(j, s_bufs[parity], bias_row)

    logits(0, sa_scr)

    def far_block(j, carry):
        step(j, True, None)
        return carry

    first = ratio * qi - 1
    lax.fori_loop(0, jnp.maximum(first, 0), far_block, 0)

    @pl.when(qi > 0)
    def _():
        step(first, True, 0)

    for t in range(ratio):
        step(first + 1 + t, t + 1 < ratio, (t + 1) * tk)

    lam = _lambda(lamp_ref, lam_init)
    for g in range(GROUP):
        a0, a1 = acc_scr[2 * g], acc_scr[2 * g + 1]
        o = a0[:V_DIM] * (1.0 / a0[V_DIM:V_DIM + 1]) - lam * (a1[:V_DIM] * (1.0 / a1[V_DIM:V_DIM + 1]))
        o = o * lax.rsqrt(jnp.mean(o * o, axis=0, keepdims=True) + NORM_EPS) * subg_ref[...] * (1.0 - lam_init)
        o_ref[:, g * V_DIM:(g + 1) * V_DIM] = o.T.astype(o_ref.dtype)


def _prompt_attention(q, k, v, rel_bias, lamp, subg, lam_init):
    s_len = q.shape[0]
    tq = tk = min(V7X_MXU_DIM, s_len)
    nq, nk = s_len // tq, s_len // tk
    ratio = tq // tk
    assert s_len % tq == 0 and tk >= MAX_DISTANCE
    qt = q.T
    kb = k.astype(BF16)
    vt = v.astype(BF16).reshape(nk, tk, N_KV_HEADS, V_DIM).transpose(2, 0, 3, 1)
    ones = jnp.zeros((N_KV_HEADS, nk, ACC_ROWS - V_DIM, tk), BF16).at[:, :, 0, :].set(1.0)
    vt = jnp.concatenate([vt, ones], axis=2)
    nb = (ratio + 1) * tk
    c = jnp.arange(nb, dtype=jnp.int32)[:, None]
    r = jnp.arange(tq, dtype=jnp.int32)[None, :]
    bias = _bias_of_distance(rel_bias, tk + r - c)
    bias = bias.reshape(N_KV_HEADS, GROUP, nb, tq).transpose(0, 2, 1, 3).reshape(N_KV_HEADS, nb, GROUP * tq)
    subg_col = subg.reshape(V_DIM, 1)
    ncomb = 2 * GROUP
    return pl.pallas_call(
        functools.partial(_prompt_attn_kernel, tq=tq, tk=tk, lam_init=lam_init),
        grid=(N_KV_HEADS, nq),
        in_specs=[
            pl.BlockSpec((GROUP * V_DIM, tq), lambda h, i: (h, i)),
            pl.BlockSpec((s_len, V_DIM), lambda h, i: (0, h)),
            pl.BlockSpec((None, nk, ACC_ROWS, tk), lambda h, i: (h, 0, 0, 0)),
            pl.BlockSpec((None, nb, GROUP * tq), lambda h, i: (h, 0, 0)),
            pl.BlockSpec(lamp.shape, lambda h, i: (0, 0)),
            pl.BlockSpec(subg_col.shape, lambda h, i: (0, 0)),
        ],
        out_specs=pl.BlockSpec((tq, GROUP * V_DIM), lambda h, i: (i, h)),
        out_shape=jax.ShapeDtypeStruct((s_len, N_HEADS * V_DIM), BF16),
        scratch_shapes=[pltpu.VMEM((V_DIM, ncomb * tq), BF16),
                        pltpu.VMEM((tk, ncomb * tq), F32), pltpu.VMEM((tk, ncomb * tq), F32),
                        pltpu.VMEM((ncomb, 1, tq), F32), pltpu.VMEM((ncomb, ACC_ROWS, tq), F32)],
        compiler_params=_params(("arbitrary", "arbitrary"), 40),
    )(qt, kb, vt, bias, lamp, subg_col)


def _decode_softmax_step(s, v_heads, m_ref, l_ref, acc_ref):
    m_old = m_ref[...]
    m_new = jnp.maximum(m_old, jnp.max(s, axis=-1, keepdims=True))
    alpha = jnp.exp(m_old - m_new)
    p = jnp.exp(s - m_new)
    l_ref[...] = alpha * l_ref[...] + jnp.sum(p, axis=-1, keepdims=True)
    pb = p.astype(BF16)
    nk = pb.shape[1] // len(v_heads)
    rows = pb.shape[0] // N_KV_HEADS
    parts = []
    for h in range(N_KV_HEADS):
        pv = None
        for i, vh in enumerate(v_heads):
            part = jnp.dot(pb[h * rows:(h + 1) * rows, i * nk:(i + 1) * nk], vh(h), preferred_element_type=F32)
            pv = part if pv is None else pv + part
        parts.append(pv)
    acc_ref[...] = alpha * acc_ref[...] + jnp.concatenate(parts, axis=0)
    m_ref[...] = m_new


def _decode_attn_kernel(pt_ref, q_ref, bias_ref, knew_ref, vnew_ref, bnew_ref, lamp_ref, subg_ref, *rest,
                        pages, page, lam_init):
    del pt_ref
    k_refs, v_refs = rest[:pages], rest[pages:2 * pages]
    o_ref, m_scr, l_scr, acc_scr = rest[2 * pages:]
    step = pl.program_id(1)

    @pl.when(step == 0)
    def _():
        m_scr[...] = jnp.full(m_scr.shape, M_INIT, F32)
        l_scr[...] = jnp.zeros(l_scr.shape, F32)
        acc_scr[...] = jnp.zeros(acc_scr.shape, F32)

    q = q_ref[...]
    rows = q.shape[0] // N_KV_HEADS
    nt = (((1,), (1,)), ((), ()))

    def head(ref, h):
        return ref[pl.ds(h, page, stride=N_KV_HEADS), :].astype(BF16)

    def logits(ref):
        return jnp.concatenate(
            [lax.dot_general(q[h * rows:(h + 1) * rows], head(ref, h), nt, preferred_element_type=F32)
             for h in range(N_KV_HEADS)], axis=0)

    s = jnp.concatenate([logits(k_refs[i]) for i in range(pages)], axis=1) + bias_ref[...]
    _decode_softmax_step(s, [functools.partial(head, v_refs[i]) for i in range(pages)], m_scr, l_scr, acc_scr)

    @pl.when(step == pl.num_programs(1) - 1)
    def _():
        s_new = logits(knew_ref) + bnew_ref[...]
        _decode_softmax_step(s_new, [functools.partial(head, vnew_ref)], m_scr, l_scr, acc_scr)
        lam = _lambda(lamp_ref, lam_init)
        out = acc_scr[...] / l_scr[...]
        for h in range(N_KV_HEADS):
            blk = out[h * rows:(h + 1) * rows]
            o = blk[:rows // 2] - lam * blk[rows // 2:]
            o = o * lax.rsqrt(jnp.mean(o * o, axis=-1, keepdims=True) + NORM_EPS) * subg_ref[...] * (1.0 - lam_init)
            o_ref[h * (rows // 2):(h + 1) * (rows // 2), :] = o.astype(o_ref.dtype)


def _decode_attention(q, k_new, v_new, cache_k, cache_v, layer, page_table, rel_bias, lamp, subg, lam_init):
    batch, n_pages = page_table.shape
    depth, n_pool, page, kvh, kdim = cache_k.shape
    t_len = q.shape[0] // batch
    rows = N_KV_HEADS * 2 * GROUP * t_len
    pages = 8
    assert n_pages % pages == 0 and page >= MAX_DISTANCE and t_len <= page
    assert kvh == N_KV_HEADS and kdim == V_DIM
    n_steps = n_pages // pages
    prow = page * kvh
    ck = cache_k.reshape(depth, n_pool, prow, kdim)
    cv = cache_v.reshape(depth, n_pool, prow, kdim)

    q6 = q.reshape(batch, t_len, N_KV_HEADS, GROUP, 2, HEAD_DIM).transpose(0, 2, 4, 3, 1, 5)
    eye_m = jnp.eye(2, dtype=q.dtype)[None, None, :, None, None, :, None]
    q_all = (q6[:, :, :, :, :, None, :] * eye_m).reshape(batch, rows, 2 * HEAD_DIM)

    row_h = (jnp.arange(N_KV_HEADS)[:, None, None, None] * GROUP + jnp.arange(GROUP)[None, None, :, None]
             + jnp.zeros((1, 2, 1, t_len), jnp.int32)).reshape(rows)
    row_t = (jnp.zeros((N_KV_HEADS, 2, GROUP, 1), jnp.int32) + jnp.arange(t_len, dtype=jnp.int32)).reshape(rows)
    key = jnp.arange(page, dtype=jnp.int32)[None, :]
    sel = jax.nn.one_hot(row_h, N_HEADS, dtype=F32)
    last = jnp.einsum('hck,ch->ck', _bias_of_distance(rel_bias, page + row_t[:, None] - key), sel)
    bias_steps = jnp.zeros((2, rows, pages * page), F32).at[1, :, (pages - 1) * page:].set(last)
    n_new = jnp.where(key < t_len, row_t[:, None] - key, -1)
    b_all = _bias_of_distance(rel_bias, n_new)
    b_new = jnp.sum(jnp.where(sel.T[:, :, None] > 0, b_all, 0.0), axis=0)
    pad = ((0, 0), (0, page - t_len), (0, 0))
    k_pad = jnp.pad(k_new.reshape(batch, t_len, kvh * kdim), pad).reshape(batch, prow, kdim)
    v_pad = jnp.pad(v_new.reshape(batch, t_len, kvh * kdim), pad).reshape(batch, prow, kdim)

    def page_spec(i):
        return pl.BlockSpec((None, None, prow, kdim), lambda b, s, pt: (layer, pt[b, s * pages + i], 0, 0))

    grid_spec = pltpu.PrefetchScalarGridSpec(
        num_scalar_prefetch=1,
        grid=(batch, n_steps),
        in_specs=[
            pl.BlockSpec((None, rows, 2 * HEAD_DIM), lambda b, s, pt: (b, 0, 0)),
            pl.BlockSpec((None, rows, pages * page), lambda b, s, pt: (jnp.where(s == n_steps - 1, 1, 0), 0, 0)),
            pl.BlockSpec((None, prow, kdim), lambda b, s, pt: (b, 0, 0)),
            pl.BlockSpec((None, prow, kdim), lambda b, s, pt: (b, 0, 0)),
            pl.BlockSpec((rows, page), lambda b, s, pt: (0, 0)),
            pl.BlockSpec(lamp.shape, lambda b, s, pt: (0, 0)),
            pl.BlockSpec(subg.shape, lambda b, s, pt: (0, 0)),
        ] + [page_spec(i) for i in range(pages)] * 2,
        out_specs=pl.BlockSpec((None, rows // 2, V_DIM), lambda b, s, pt: (b, 0, 0)),
        scratch_shapes=[pltpu.VMEM((rows, 1), F32), pltpu.VMEM((rows, 1), F32), pltpu.VMEM((rows, V_DIM), F32)],
    )
    out = pl.pallas_call(
        functools.partial(_decode_attn_kernel, pages=pages, page=page, lam_init=lam_init),
        grid_spec=grid_spec,
        out_shape=jax.ShapeDtypeStruct((batch, rows // 2, V_DIM), BF16),
        compiler_params=_params(("arbitrary", "arbitrary"), 32),
    )(page_table, q_all, bias_steps, k_pad, v_pad, b_new, lamp, subg, *([ck] * pages), *([cv] * pages))
    out = out.reshape(batch, N_KV_HEADS, GROUP, t_len, V_DIM).transpose(0, 3, 1, 2, 4)
    return out.reshape(batch * t_len, N_HEADS * V_DIM)


def _gelu_tanh(x):
    return 0.5 * x * (1.0 + jnp.tanh(math.sqrt(2.0 / math.pi) * (x + 0.044715 * (x * x * x))))


def _lru_kernel(xr_ref, yr_ref, h0_ref, buf0_ref, cw_ref, cb_ref, wa_ref, ba_ref, wx_ref, bx_ref, lam_ref,
                o_ref, hlast_ref, tail_ref, xp_scr, h_scr, *, tb):
    pad = V7X_SUBLANES
    tail = CONV_WIDTH - 1

    @pl.when(pl.program_id(1) == 0)
    def _():
        h_scr[...] = h0_ref[...]
        xp_scr[pad - tail:pad, :] = buf0_ref[...]

    x = xr_ref[...]
    xp_scr[pad:pad + tb, :] = x
    cw = cw_ref[...]
    xc = cb_ref[...] + xp_scr[pad - tail:pad - tail + tb, :] * cw[0:1]
    for j in range(1, CONV_WIDTH):
        xc = xc + xp_scr[pad - tail + j:pad - tail + j + tb, :] * cw[j:j + 1]
    new_tail = xp_scr[pad + tb - tail:pad + tb, :]
    tail_ref[...] = new_tail
    xp_scr[pad - tail:pad, :] = new_tail

    xcb = xc.astype(BF16)
    nblk, bw = wa_ref.shape[0], wa_ref.shape[1]

    def gate(w_ref, b_ref):
        parts = [jnp.dot(xcb[:, c * bw:(c + 1) * bw], w_ref[c], preferred_element_type=F32) for c in range(nblk)]
        return _sigmoid(jnp.concatenate(parts, axis=1) + b_ref[...])

    r = gate(wa_ref, ba_ref)
    i = gate(wx_ref, bx_ref)
    neg_lam = -lam_ref[...]
    softplus = jnp.maximum(neg_lam, 0.0) + jnp.log(1.0 + jnp.exp(-jnp.abs(neg_lam)))
    log_a = -LRU_C * r * softplus
    a = jnp.exp(log_a)
    u = jnp.sqrt(-jnp.tanh(log_a) * (a * a + 1.0)) * i * xc

    row = lax.broadcasted_iota(jnp.int32, a.shape, 0)
    big_a, big_b = a, u
    shift = 1
    while shift < tb:
        keep = row >= shift
        a_prev = jnp.where(keep, pltpu.roll(big_a, shift, 0), 1.0)
        b_prev = jnp.where(keep, pltpu.roll(big_b, shift, 0), 0.0)
        big_b = big_a * b_prev + big_b
        big_a = big_a * a_prev
        shift *= 2
    h = big_a * h_scr[...] + big_b
    h_scr[...] = h[tb - 1:tb]
    hlast_ref[...] = h[tb - 1:tb]
    o_ref[...] = (h * _gelu_tanh(yr_ref[...])).astype(o_ref.dtype)


def _lru(xr, yr, h0, buf0, conv_w, conv_b, wa, ba, wx, bx, lam, *, batch):
    m, w = xr.shape
    t_len = m // batch
    tb = min(256, t_len)
    nt = t_len // tb
    assert t_len % tb == 0 and tb % V7X_SUBLANES == 0
    tail = CONV_WIDTH - 1
    xr3, yr3 = xr.reshape(batch, t_len, w), yr.reshape(batch, t_len, w)
    rows = pl.BlockSpec((None, tb, w), lambda b, t: (b, t, 0))
    vec = pl.BlockSpec((1, w), lambda b, t: (0, 0))
    wspec = pl.BlockSpec(wa.shape, lambda b, t: (0, 0, 0))
    out, h_last, conv_tail = pl.pallas_call(
        functools.partial(_lru_kernel, tb=tb),
        grid=(batch, nt),
        in_specs=[rows, rows,
                  pl.BlockSpec((None, 1, w), lambda b, t: (b, 0, 0)),
                  pl.BlockSpec((None, tail, w), lambda b, t: (b, 0, 0)),
                  pl.BlockSpec((CONV_WIDTH, w), lambda b, t: (0, 0)), vec, wspec, vec, wspec, vec, vec],
        out_specs=[rows,
                   pl.BlockSpec((None, 1, w), lambda b, t: (b, 0, 0)),
                   pl.BlockSpec((None, tail, w), lambda b, t: (b, 0, 0))],
        out_shape=[jax.ShapeDtypeStruct((batch, t_len, w), BF16),
                   jax.ShapeDtypeStruct((batch, 1, w), F32),
                   jax.ShapeDtypeStruct((batch, tail, w), F32)],
        scratch_shapes=[pltpu.VMEM((V7X_SUBLANES + tb, w), F32), pltpu.VMEM((1, w), F32)],
        compiler_params=_params(("arbitrary", "arbitrary"), 48),
    )(xr3, yr3, h0.reshape(batch, 1, w), buf0, conv_w, conv_b, wa, ba, wx, bx, lam)
    return out.reshape(m, w), h_last.reshape(batch, w), conv_tail


def _block_diag(w, per):
    n, d, e = w.shape
    wg = w.reshape(n // per, per, d, e)
    eye = jnp.eye(per, dtype=w.dtype)
    return jnp.einsum('gpde,pq->gpdqe', wg, eye).reshape(n // per, per * d, per * e)


def _merge_kernel(x_ref, attn_ref, lru_ref, gla_ref, glr_ref, bga_ref, bgr_ref, g_ref, wpa_ref, wpr_ref, wo_ref,
                  o_ref):
    pa = jnp.dot(attn_ref[...], wpa_ref[...], preferred_element_type=F32)
    pr = jnp.dot(lru_ref[...], wpr_ref[...], preferred_element_type=F32)
    merged = _sigmoid(gla_ref[...] + bga_ref[...]) * pa + _sigmoid(glr_ref[...] + bgr_ref[...]) * pr
    o_ref[...] = x_ref[...] + g_ref[...] * jnp.dot(merged.astype(BF16), wo_ref[...], preferred_element_type=F32)


def _merge(x, attn, lru_out, gl, b_gate, g, w_pa, w_pr, w_o):
    m, d = x.shape
    tm = min(512, m)
    assert m % tm == 0
    row = pl.BlockSpec((tm, d), lambda i: (i, 0))
    wspec = _resident((d, d), lambda i: (0, 0))
    return pl.pallas_call(
        _merge_kernel,
        grid=(m // tm,),
        in_specs=[row, row, row, row, pl.BlockSpec((tm, d), lambda i: (i, 1)),
                  pl.BlockSpec((1, d), lambda i: (0, 0)), pl.BlockSpec((1, d), lambda i: (0, 1)),
                  _mod_spec(g, tm), wspec, wspec, wspec],
        out_specs=row,
        out_shape=jax.ShapeDtypeStruct((m, d), F32),
        compiler_params=_params(("arbitrary",), 48),
    )(x, attn, lru_out, gl, gl, b_gate, b_gate, g, w_pa, w_pr, w_o)


def _run_trunk(x, mods, p, *, batch, cache=None):
    depth = p['w_in'].shape[0]
    d = x.shape[1]
    ks, vs, hs, convs = [], [], [], []
    for l in range(depth):
        sh1, sc1, g1, sh2, sc2, g2, sh3, sc3, g3 = mods[l]
        lam_init = 0.8 - 0.6 * math.exp(-0.3 * l)
        lamp = p['lamp'][l]
        subg = p['attn_sub_g'][l][None, :]
        x = _ffn(x, sh1, sc1, g1, p['ln_ffn1'][l][None], p['ln_final'][None], p['ffn1_gu'][l], p['ffn1_down'][l],
                 final_norm=False)
        q, k, v, xr, yr, gl = _mixin(x, sh2, sc2, p['ln_mix'][l][None], p['w_in'][l])
        if cache is None:
            attn = _prompt_attention(q, k, v, p['rel_bias'], lamp, subg, lam_init)
            h0 = jnp.zeros((batch, d), F32)
            buf0 = jnp.zeros((batch, CONV_WIDTH - 1, d), F32)
        else:
            cache_k, cache_v, page_table, state_h, state_conv = cache
            attn = _decode_attention(q, k, v, cache_k, cache_v, l, page_table, p['rel_bias'], lamp, subg, lam_init)
            h0, buf0 = state_h[l], state_conv[l]
        lru_out, h_last, conv_tail = _lru(xr, yr, h0, buf0, p['conv_w'][l], p['conv_b'][l][None], p['rg_wa'][l],
                                          p['rg_ba'][l][None], p['rg_wx'][l], p['rg_bx'][l][None],
                                          p['rg_lambda'][l][None], batch=batch)
        x = _merge(x, attn, lru_out, gl, p['b_gate'][l][None], g2, p['w_pa'][l], p['w_pr'][l], p['w_o'][l])
        x = _ffn(x, sh3, sc3, g3, p['ln_ffn2'][l][None], p['ln_final'][None], p['ffn2_gu'][l], p['ffn2_down'][l],
                 final_norm=(l == depth - 1))
        t_len = x.shape[0] // batch
        ks.append(k.reshape(batch, t_len, N_KV_HEADS, 2 * HEAD_DIM))
        vs.append(v.reshape(batch, t_len, N_KV_HEADS, V_DIM))
        hs.append(h_last)
        convs.append(conv_tail)
    return x, jnp.stack(ks), jnp.stack(vs), jnp.stack(hs), jnp.stack(convs)


def kernel(x_prompt, x_sample, cache_k, cache_v, state_h, state_conv, page_table, c_prompt, c_sample, rel_bias, ada_w, ada_b, ln_ffn1, ffn1_gu, ffn1_down, ln_mix, w_in, b_gate, lambda_q1, lambda_k1, lambda_q2, lambda_k2, attn_sub_g, conv_w, conv_b, rg_wa, rg_ba, rg_wx, rg_bx, rg_lambda, w_pa, w_pr, w_o, ln_ffn2, ffn2_gu, ffn2_down, ln_final):
    b_p, s_len, d = x_prompt.shape
    b_d, t_dec, _ = x_sample.shape
    depth = w_in.shape[0]
    per = V7X_MXU_DIM // (rg_wa.shape[-1])
    p = dict(
        rel_bias=rel_bias, ln_ffn1=ln_ffn1, ln_mix=ln_mix, ln_ffn2=ln_ffn2, ln_final=ln_final, b_gate=b_gate,
        attn_sub_g=attn_sub_g, conv_w=conv_w, conv_b=conv_b, rg_ba=rg_ba, rg_bx=rg_bx, rg_lambda=rg_lambda,
        ffn1_gu=ffn1_gu.astype(BF16), ffn1_down=ffn1_down.astype(BF16), w_in=w_in.astype(BF16),
        w_pa=w_pa.astype(BF16), w_pr=w_pr.astype(BF16), w_o=w_o.astype(BF16),
        ffn2_gu=ffn2_gu.astype(BF16), ffn2_down=ffn2_down.astype(BF16),
        rg_wa=jnp.stack([_block_diag(rg_wa[l].astype(BF16), per) for l in range(depth)]),
        rg_wx=jnp.stack([_block_diag(rg_wx[l].astype(BF16), per) for l in range(depth)]),
        lamp=jnp.stack([lambda_q1, lambda_k1, lambda_q2, lambda_k2], axis=1).astype(F32),
    )

    n_seq = b_p + b_d
    pad = (-n_seq) % V7X_SUBLANES
    c_all = jnp.pad(jnp.concatenate([c_prompt, c_sample], axis=0), ((0, pad), (0, 0)))
    mods = _ada_mods(c_all, ada_w, ada_b)

    def split(rows, repeat):
        out = []
        for l in range(depth):
            parts = [rows[l][:, k * d:(k + 1) * d] for k in range(N_MOD)]
            out.append([jnp.repeat(m, repeat, axis=0) if repeat > 1 else m for m in parts])
        return out

    assert b_p == 1
    mods_p = split(mods[:, :b_p], 1)
    mods_d = split(mods[:, b_p:n_seq], t_dec)

    y_p, k_p, v_p, h_p, conv_p = _run_trunk(x_prompt.reshape(b_p * s_len, d), mods_p, p, batch=b_p)
    y_d, k_d, v_d, h_d, conv_d = _run_trunk(x_sample.reshape(b_d * t_dec, d), mods_d, p, batch=b_d,
                                            cache=(cache_k, cache_v, page_table, state_h, state_conv))
    return (y_p.reshape(b_p, s_len, d), y_d.reshape(b_d, t_dec, d), k_p, v_p, h_p, conv_p, k_d, v_d, h_d, conv_d)
```

```python
import functools
import math

import jax
import jax.numpy as jnp
import numpy as np
from jax import lax
from jax.experimental import pallas as pl
from jax.experimental.pallas import tpu as pltpu

F32 = jnp.float32
BF16 = jnp.bfloat16

N_HEADS = 8
N_KV_HEADS = 4
GROUP = N_HEADS // N_KV_HEADS
HEAD_DIM = 64
V_DIM = 2 * HEAD_DIM
LRU_BLOCKS = 16
CONV_WIDTH = 4
LRU_C = 8.0
N_BUCKETS = 32
MAX_DISTANCE = 128
NORM_EPS = 1e-6
N_MOD = 9

V7X_LANES = 128
V7X_SUBLANES = 8
V7X_MXU_DIM = 256
V7X_VMEM_BYTES = 64 * 1024 * 1024
MIB = 1024 * 1024


def _params(semantics, vmem_mib):
    assert vmem_mib * MIB < V7X_VMEM_BYTES
    return pltpu.CompilerParams(dimension_semantics=semantics, vmem_limit_bytes=vmem_mib * MIB)


def _resident(shape, index_map):
    return pl.BlockSpec(shape, index_map, pipeline_mode=pl.Buffered(1))


def _sigmoid(x):
    return 1.0 / (1.0 + jnp.exp(-x))


def _modulated_norm(x, ln, sc, sh):
    y = x * lax.rsqrt(jnp.mean(x * x, axis=-1, keepdims=True) + NORM_EPS) * ln
    return y * (1.0 + sc) + sh


def _mod_spec(mod, tm):
    if mod.shape[0] == 1:
        return pl.BlockSpec((1, mod.shape[1]), lambda i: (0, 0))
    return pl.BlockSpec((tm, mod.shape[1]), lambda i: (i, 0))


def _ada_kernel(c_ref, w_ref, b_ref, o_ref):
    c = c_ref[...]
    act = c * _sigmoid(c)
    o_ref[...] = jnp.dot(act, w_ref[...], preferred_element_type=F32) + b_ref[...]


def _ada_mods(c_all, ada_w, ada_b):
    depth, d, n = ada_w.shape
    mp = c_all.shape[0]
    tn = n // 8
    assert tn % V7X_LANES == 0 and mp % V7X_SUBLANES == 0
    return pl.pallas_call(
        _ada_kernel,
        grid=(depth, n // tn),
        in_specs=[
            pl.BlockSpec((mp, d), lambda l, j: (0, 0)),
            pl.BlockSpec((None, d, tn), lambda l, j: (l, 0, j)),
            pl.BlockSpec((None, 1, tn), lambda l, j: (l, 0, j)),
        ],
        out_specs=pl.BlockSpec((None, mp, tn), lambda l, j: (l, 0, j)),
        out_shape=jax.ShapeDtypeStruct((depth, mp, n), F32),
        compiler_params=_params(("arbitrary", "arbitrary"), 24),
    )(c_all, ada_w, ada_b.reshape(depth, 1, n))


def _ffn_kernel(x_ref, sh_ref, sc_ref, g_ref, ln_ref, lnf_ref, wgu_ref, wd_ref, o_ref, *, d_ff, tf, final_norm):
    x = x_ref[...]
    h = _modulated_norm(x, ln_ref[...], sc_ref[...], sh_ref[...]).astype(BF16)
    acc = jnp.zeros(x.shape, F32)
    for c in range(d_ff // tf):
        gate = jnp.dot(h, wgu_ref[:, c * tf:(c + 1) * tf], preferred_element_type=F32)
        up = jnp.dot(h, wgu_ref[:, d_ff + c * tf:d_ff + (c + 1) * tf], preferred_element_type=F32)
        act = (gate * _sigmoid(gate) * up).astype(BF16)
        acc = acc + jnp.dot(act, wd_ref[c * tf:(c + 1) * tf, :], preferred_element_type=F32)
    y = x + 0.5 * g_ref[...] * acc
    if final_norm:
        y = y * lax.rsqrt(jnp.mean(y * y, axis=-1, keepdims=True) + NORM_EPS) * lnf_ref[...]
    o_ref[...] = y


def _ffn(x, sh, sc, g, ln, lnf, w_gu, w_down, *, final_norm):
    m, d = x.shape
    d_ff = w_down.shape[0]
    tm = min(512, m)
    tf = V7X_MXU_DIM
    assert m % tm == 0 and d_ff % tf == 0
    row = pl.BlockSpec((tm, d), lambda i: (i, 0))
    vec = pl.BlockSpec((1, d), lambda i: (0, 0))
    return pl.pallas_call(
        functools.partial(_ffn_kernel, d_ff=d_ff, tf=tf, final_norm=final_norm),
        grid=(m // tm,),
        in_specs=[row, _mod_spec(sh, tm), _mod_spec(sc, tm), _mod_spec(g, tm), vec, vec,
                  _resident(w_gu.shape, lambda i: (0, 0)), _resident(w_down.shape, lambda i: (0, 0))],
        out_specs=row,
        out_shape=jax.ShapeDtypeStruct((m, d), F32),
        compiler_params=_params(("arbitrary",), 48),
    )(x, sh, sc, g, ln, lnf, w_gu, w_down)


def _mixin_kernel(x_ref, sh_ref, sc_ref, ln_ref, w_ref, q_ref, k_ref, v_ref, xr_ref, yr_ref, gl_ref):
    h = _modulated_norm(x_ref[...], ln_ref[...], sc_ref[...], sh_ref[...]).astype(BF16)
    col = 0
    for ref, scale in ((q_ref, HEAD_DIM ** -0.5), (k_ref, None), (v_ref, None), (xr_ref, None), (yr_ref, None),
                       (gl_ref, None)):
        width = ref.shape[1]
        y = jnp.dot(h, w_ref[:, col:col + width], preferred_element_type=F32)
        if scale is not None:
            y = y * scale
        ref[...] = y.astype(ref.dtype)
        col += width


def _mixin(x, sh, sc, ln, w_in):
    m, d = x.shape
    q_cols, kv_cols = N_HEADS * 2 * HEAD_DIM, N_KV_HEADS * 2 * HEAD_DIM
    tm = min(256, m)
    assert m % tm == 0 and w_in.shape[1] == q_cols + 2 * kv_cols + 4 * d
    widths = (q_cols, kv_cols, kv_cols, d, d, 2 * d)
    dtypes = (BF16, F32, F32, F32, F32, F32)
    return pl.pallas_call(
        _mixin_kernel,
        grid=(m // tm,),
        in_specs=[pl.BlockSpec((tm, d), lambda i: (i, 0)), _mod_spec(sh, tm), _mod_spec(sc, tm),
                  pl.BlockSpec((1, d), lambda i: (0, 0)), _resident(w_in.shape, lambda i: (0, 0))],
        out_specs=[pl.BlockSpec((tm, w), lambda i: (i, 0)) for w in widths],
        out_shape=[jax.ShapeDtypeStruct((m, w), dt) for w, dt in zip(widths, dtypes)],
        compiler_params=_params(("arbitrary",), 48),
    )(x, sh, sc, ln, w_in)


def _rel_bucket(n):
    max_exact = N_BUCKETS // 2
    nf = jnp.maximum(n, 1).astype(F32)
    large = max_exact + (jnp.log(nf / max_exact) / math.log(MAX_DISTANCE / max_exact)
                         * (N_BUCKETS - max_exact)).astype(jnp.int32)
    large = jnp.minimum(large, N_BUCKETS - 1)
    return jnp.where(n < max_exact, n, large)


def _bias_of_distance(rel_bias, n):
    table = rel_bias.astype(F32) - rel_bias[N_BUCKETS - 1].astype(F32)
    onehot = (_rel_bucket(jnp.maximum(n, 0))[..., None] == jnp.arange(N_BUCKETS, dtype=jnp.int32)).astype(F32)
    b = jnp.einsum('...b,bh->h...', onehot, table, precision=lax.Precision.HIGHEST)
    return jnp.where(n >= 0, b, -jnp.inf)


def _lambda(lamp_ref, lam_init):
    lp = lamp_ref[...]
    return jnp.exp(jnp.sum(lp[0:1] * lp[1:2])) - jnp.exp(jnp.sum(lp[2:3] * lp[3:4])) + lam_init


M_INIT = -1e30


ACC_ROWS = V_DIM + 16
FAR_UNROLL = 4


def _col_max(s):
    rows = s.shape[0]
    while rows > V7X_SUBLANES:
        rows //= 2
        s = jnp.maximum(s[:rows], s[rows:])
    return jnp.max(s, axis=0, keepdims=True)


def _prompt_attn_kernel(qt_ref, k_ref, vt_ref, bias_ref, lamp_ref, subg_ref, o_ref,
                        qz_scr, sa_scr, sb_scr, m_scr, acc_scr, *, tq, tk, lam_init):
    qi = pl.program_id(1)
    ncomb = 2 * GROUP
    ratio = tq // tk
    row = lax.broadcasted_iota(jnp.int32, (V_DIM, tq), 0)
    for g in range(GROUP):
        qg = qt_ref[g * V_DIM:(g + 1) * V_DIM, :]
        qz_scr[:, (2 * g) * tq:(2 * g + 1) * tq] = jnp.where(row < HEAD_DIM, qg, jnp.zeros_like(qg))
        qz_scr[:, (2 * g + 1) * tq:(2 * g + 2) * tq] = jnp.where(row >= HEAD_DIM, qg, jnp.zeros_like(qg))
    m_scr[...] = jnp.full(m_scr.shape, M_INIT, F32)
    acc_scr[...] = jnp.zeros(acc_scr.shape, F32)
    s_bufs = (sa_scr, sb_scr)

    def logits(j, dst):
        kb = k_ref[pl.ds(pl.multiple_of(j * tk, tk), tk), :]
        dst[...] = jnp.dot(kb, qz_scr[...], preferred_element_type=F32)

    def update(j, src, bias_row):
        vt = vt_ref[j]
        for c in range(ncomb):
            s = src[:, c * tq:(c + 1) * tq]
            if bias_row is not None:
                g = c // 2
                s = s + bias_ref[bias_row:bias_row + tk, g * tq:(g + 1) * tq]
            m_old = m_scr[c]
            m_new = jnp.maximum(m_old, _col_max(s))
            alpha = jnp.exp(m_old - m_new)
            p = jnp.exp(s - m_new).astype(BF16)
            acc_scr[c] = alpha * acc_scr[c] + jnp.dot(vt, p, preferred_element_type=F32)
            m_scr[c] = m_new

    def step(j, with_next, bias_row):
        for parity in range(2):
            @pl.when(j % 2 == parity)
            def _():
                if with_next:
                    logits(j + 1, s_bufs[1 - parity])
                update(j, s_bufs[parity], bias_row)

    logits(0, sa_scr)

    def far_block(j, carry):
        step(j, True, None)
        return carry

    def far_blocks(t, carry):
        j = t * FAR_UNROLL
        for u in range(FAR_UNROLL):
            logits(j + u + 1, s_bufs[(u + 1) % 2])
            update(j + u, s_bufs[u % 2], None)
        return carry

    first = ratio * qi - 1
    n_far = jnp.maximum(first, 0)
    n_runs = n_far // FAR_UNROLL
    lax.fori_loop(0, n_runs, far_blocks, 0)
    lax.fori_loop(n_runs * FAR_UNROLL, n_far, far_block, 0)

    @pl.when(qi > 0)
    def _():
        step(first, True, 0)

    for t in range(ratio):
        step(first + 1 + t, t + 1 < ratio, (t + 1) * tk)

    lam = _lambda(lamp_ref, lam_init)
    for g in range(GROUP):
        a0, a1 = acc_scr[2 * g], acc_scr[2 * g + 1]
        o = a0[:V_DIM] * (1.0 / a0[V_DIM:V_DIM + 1]) - lam * (a1[:V_DIM] * (1.0 / a1[V_DIM:V_DIM + 1]))
        o = o * lax.rsqrt(jnp.mean(o * o, axis=0, keepdims=True) + NORM_EPS) * subg_ref[...] * (1.0 - lam_init)
        o_ref[:, g * V_DIM:(g + 1) * V_DIM] = o.T.astype(o_ref.dtype)


def _prompt_attention(q, k, v, rel_bias, lamp, subg, lam_init):
    s_len = q.shape[0]
    tq = tk = min(V7X_MXU_DIM, s_len)
    nq, nk = s_len // tq, s_len // tk
    ratio = tq // tk
    assert s_len % tq == 0 and tk >= MAX_DISTANCE
    qt = q.T
    kb = k.astype(BF16)
    vt = v.astype(BF16).reshape(nk, tk, N_KV_HEADS, V_DIM).transpose(2, 0, 3, 1)
    ones = jnp.zeros((N_KV_HEADS, nk, ACC_ROWS - V_DIM, tk), BF16).at[:, :, 0, :].set(1.0)
    vt = jnp.concatenate([vt, ones], axis=2)
    nb = (ratio + 1) * tk
    c = jnp.arange(nb, dtype=jnp.int32)[:, None]
    r = jnp.arange(tq, dtype=jnp.int32)[None, :]
    bias = _bias_of_distance(rel_bias, tk + r - c)
    bias = bias.reshape(N_KV_HEADS, GROUP, nb, tq).transpose(0, 2, 1, 3).reshape(N_KV_HEADS, nb, GROUP * tq)
    subg_col = subg.reshape(V_DIM, 1)
    ncomb = 2 * GROUP
    return pl.pallas_call(
        functools.partial(_prompt_attn_kernel, tq=tq, tk=tk, lam_init=lam_init),
        grid=(N_KV_HEADS, nq),
        in_specs=[
            pl.BlockSpec((GROUP * V_DIM, tq), lambda h, i: (h, i)),
            pl.BlockSpec((s_len, V_DIM), lambda h, i: (0, h)),
            pl.BlockSpec((None, nk, ACC_ROWS, tk), lambda h, i: (h, 0, 0, 0)),
            pl.BlockSpec((None, nb, GROUP * tq), lambda h, i: (h, 0, 0)),
            pl.BlockSpec(lamp.shape, lambda h, i: (0, 0)),
            pl.BlockSpec(subg_col.shape, lambda h, i: (0, 0)),
        ],
        out_specs=pl.BlockSpec((tq, GROUP * V_DIM), lambda h, i: (i, h)),
        out_shape=jax.ShapeDtypeStruct((s_len, N_HEADS * V_DIM), BF16),
        scratch_shapes=[pltpu.VMEM((V_DIM, ncomb * tq), BF16),
                        pltpu.VMEM((tk, ncomb * tq), F32), pltpu.VMEM((tk, ncomb * tq), F32),
                        pltpu.VMEM((ncomb, 1, tq), F32), pltpu.VMEM((ncomb, ACC_ROWS, tq), F32)],
        compiler_params=_params(("arbitrary", "arbitrary"), 40),
    )(qt, kb, vt, bias, lamp, subg_col)


def _decode_softmax_step(s, v_heads, m_ref, l_ref, acc_ref):
    m_old = m_ref[...]
    m_new = jnp.maximum(m_old, jnp.max(s, axis=-1, keepdims=True))
    alpha = jnp.exp(m_old - m_new)
    p = jnp.exp(s - m_new)
    l_ref[...] = alpha * l_ref[...] + jnp.sum(p, axis=-1, keepdims=True)
    pb = p.astype(BF16)
    nk = pb.shape[1] // len(v_heads)
    rows = pb.shape[0] // N_KV_HEADS
    parts = []
    for h in range(N_KV_HEADS):
        pv = None
        for i, vh in enumerate(v_heads):
            part = jnp.dot(pb[h * rows:(h + 1) * rows, i * nk:(i + 1) * nk], vh(h), preferred_element_type=F32)
            pv = part if pv is None else pv + part
        parts.append(pv)
    acc_ref[...] = alpha * acc_ref[...] + jnp.concatenate(parts, axis=0)
    m_ref[...] = m_new


def _decode_attn_kernel(pt_ref, q_ref, bias_ref, knew_ref, vnew_ref, bnew_ref, lamp_ref, subg_ref, *rest,
                        pages, page, lam_init):
    del pt_ref
    k_refs, v_refs = rest[:pages], rest[pages:2 * pages]
    o_ref, m_scr, l_scr, acc_scr = rest[2 * pages:]
    step = pl.program_id(1)

    @pl.when(step == 0)
    def _():
        m_scr[...] = jnp.full(m_scr.shape, M_INIT, F32)
        l_scr[...] = jnp.zeros(l_scr.shape, F32)
        acc_scr[...] = jnp.zeros(acc_scr.shape, F32)

    q = q_ref[...]
    rows = q.shape[0] // N_KV_HEADS
    nt = (((1,), (1,)), ((), ()))

    def head(ref, h):
        return ref[pl.ds(h, page, stride=N_KV_HEADS), :].astype(BF16)

    def logits(ref):
        return jnp.concatenate(
            [lax.dot_general(q[h * rows:(h + 1) * rows], head(ref, h), nt, preferred_element_type=F32)
             for h in range(N_KV_HEADS)], axis=0)

    s = jnp.concatenate([logits(k_refs[i]) for i in range(pages)], axis=1) + bias_ref[...]
    _decode_softmax_step(s, [functools.partial(head, v_refs[i]) for i in range(pages)], m_scr, l_scr, acc_scr)

    @pl.when(step == pl.num_programs(1) - 1)
    def _():
        s_new = logits(knew_ref) + bnew_ref[...]
        _decode_softmax_step(s_new, [functools.partial(head, vnew_ref)], m_scr, l_scr, acc_scr)
        lam = _lambda(lamp_ref, lam_init)
        out = acc_scr[...] / l_scr[...]
        for h in range(N_KV_HEADS):
            blk = out[h * rows:(h + 1) * rows]
            o = blk[:rows // 2] - lam * blk[rows // 2:]
            o = o * lax.rsqrt(jnp.mean(o * o, axis=-1, keepdims=True) + NORM_EPS) * subg_ref[...] * (1.0 - lam_init)
            o_ref[h * (rows // 2):(h + 1) * (rows // 2), :] = o.astype(o_ref.dtype)


def _decode_attention(q, k_new, v_new, cache_k, cache_v, layer, page_table, rel_bias, lamp, subg, lam_init):
    batch, n_pages = page_table.shape
    depth, n_pool, page, kvh, kdim = cache_k.shape
    t_len = q.shape[0] // batch
    rows = N_KV_HEADS * 2 * GROUP * t_len
    pages = 16
    assert n_pages % pages == 0 and page >= MAX_DISTANCE and t_len <= page
    assert kvh == N_KV_HEADS and kdim == V_DIM
    n_steps = n_pages // pages
    prow = page * kvh
    ck = cache_k.reshape(depth, n_pool, prow, kdim)
    cv = cache_v.reshape(depth, n_pool, prow, kdim)

    q6 = q.reshape(batch, t_len, N_KV_HEADS, GROUP, 2, HEAD_DIM).transpose(0, 2, 4, 3, 1, 5)
    eye_m = jnp.eye(2, dtype=q.dtype)[None, None, :, None, None, :, None]
    q_all = (q6[:, :, :, :, :, None, :] * eye_m).reshape(batch, rows, 2 * HEAD_DIM)

    row_h = (jnp.arange(N_KV_HEADS)[:, None, None, None] * GROUP + jnp.arange(GROUP)[None, None, :, None]
             + jnp.zeros((1, 2, 1, t_len), jnp.int32)).reshape(rows)
    row_t = (jnp.zeros((N_KV_HEADS, 2, GROUP, 1), jnp.int32) + jnp.arange(t_len, dtype=jnp.int32)).reshape(rows)
    key = jnp.arange(page, dtype=jnp.int32)[None, :]
    sel = jax.nn.one_hot(row_h, N_HEADS, dtype=F32)
    last = jnp.einsum('hck,ch->ck', _bias_of_distance(rel_bias, page + row_t[:, None] - key), sel)
    bias_steps = jnp.zeros((2, rows, pages * page), F32).at[1, :, (pages - 1) * page:].set(last)
    n_new = jnp.where(key < t_len, row_t[:, None] - key, -1)
    b_all = _bias_of_distance(rel_bias, n_new)
    b_new = jnp.sum(jnp.where(sel.T[:, :, None] > 0, b_all, 0.0), axis=0)
    pad = ((0, 0), (0, page - t_len), (0, 0))
    k_pad = jnp.pad(k_new.reshape(batch, t_len, kvh * kdim), pad).reshape(batch, prow, kdim)
    v_pad = jnp.pad(v_new.reshape(batch, t_len, kvh * kdim), pad).reshape(batch, prow, kdim)

    def page_spec(i):
        return pl.BlockSpec((None, None, prow, kdim), lambda b, s, pt: (layer, pt[b, s * pages + i], 0, 0))

    grid_spec = pltpu.PrefetchScalarGridSpec(
        num_scalar_prefetch=1,
        grid=(batch, n_steps),
        in_specs=[
            pl.BlockSpec((None, rows, 2 * HEAD_DIM), lambda b, s, pt: (b, 0, 0)),
            pl.BlockSpec((None, rows, pages * page), lambda b, s, pt: (jnp.where(s == n_steps - 1, 1, 0), 0, 0)),
            pl.BlockSpec((None, prow, kdim), lambda b, s, pt: (b, 0, 0)),
            pl.BlockSpec((None, prow, kdim), lambda b, s, pt: (b, 0, 0)),
            pl.BlockSpec((rows, page), lambda b, s, pt: (0, 0)),
            pl.BlockSpec(lamp.shape, lambda b, s, pt: (0, 0)),
            pl.BlockSpec(subg.shape, lambda b, s, pt: (0, 0)),
        ] + [page_spec(i) for i in range(pages)] * 2,
        out_specs=pl.BlockSpec((None, rows // 2, V_DIM), lambda b, s, pt: (b, 0, 0)),
        scratch_shapes=[pltpu.VMEM((rows, 1), F32), pltpu.VMEM((rows, 1), F32), pltpu.VMEM((rows, V_DIM), F32)],
    )
    out = pl.pallas_call(
        functools.partial(_decode_attn_kernel, pages=pages, page=page, lam_init=lam_init),
        grid_spec=grid_spec,
        out_shape=jax.ShapeDtypeStruct((batch, rows // 2, V_DIM), BF16),
        compiler_params=_params(("arbitrary", "arbitrary"), 32),
    )(page_table, q_all, bias_steps, k_pad, v_pad, b_new, lamp, subg, *([ck] * pages), *([cv] * pages))
    out = out.reshape(batch, N_KV_HEADS, GROUP, t_len, V_DIM).transpose(0, 3, 1, 2, 4)
    return out.reshape(batch * t_len, N_HEADS * V_DIM)


def _gelu_tanh(x):
    return 0.5 * x * (1.0 + jnp.tanh(math.sqrt(2.0 / math.pi) * (x + 0.044715 * (x * x * x))))


def _lru_kernel(xr_ref, yr_ref, h0_ref, buf0_ref, cw_ref, cb_ref, wa_ref, ba_ref, wx_ref, bx_ref, lam_ref,
                o_ref, hlast_ref, tail_ref, xp_scr, h_scr, *, tb):
    pad = V7X_SUBLANES
    tail = CONV_WIDTH - 1

    @pl.when(pl.program_id(1) == 0)
    def _():
        h_scr[...] = h0_ref[...]
        xp_scr[pad - tail:pad, :] = buf0_ref[...]

    x = xr_ref[...]
    xp_scr[pad:pad + tb, :] = x
    cw = cw_ref[...]
    xc = cb_ref[...] + xp_scr[pad - tail:pad - tail + tb, :] * cw[0:1]
    for j in range(1, CONV_WIDTH):
        xc = xc + xp_scr[pad - tail + j:pad - tail + j + tb, :] * cw[j:j + 1]
    new_tail = xp_scr[pad + tb - tail:pad + tb, :]
    tail_ref[...] = new_tail
    xp_scr[pad - tail:pad, :] = new_tail

    xcb = xc.astype(BF16)
    nblk, bw = wa_ref.shape[0], wa_ref.shape[1]

    def gate(w_ref, b_ref):
        parts = [jnp.dot(xcb[:, c * bw:(c + 1) * bw], w_ref[c], preferred_element_type=F32) for c in range(nblk)]
        return _sigmoid(jnp.concatenate(parts, axis=1) + b_ref[...])

    r = gate(wa_ref, ba_ref)
    i = gate(wx_ref, bx_ref)
    neg_lam = -lam_ref[...]
    softplus = jnp.maximum(neg_lam, 0.0) + jnp.log(1.0 + jnp.exp(-jnp.abs(neg_lam)))
    log_a = -LRU_C * r * softplus
    a = jnp.exp(log_a)
    u = jnp.sqrt(-jnp.tanh(log_a) * (a * a + 1.0)) * i * xc

    row = lax.broadcasted_iota(jnp.int32, a.shape, 0)
    big_a, big_b = a, u
    shift = 1
    while shift < tb:
        keep = row >= shift
        a_prev = jnp.where(keep, pltpu.roll(big_a, shift, 0), 1.0)
        b_prev = jnp.where(keep, pltpu.roll(big_b, shift, 0), 0.0)
        big_b = big_a * b_prev + big_b
        big_a = big_a * a_prev
        shift *= 2
    h = big_a * h_scr[...] + big_b
    h_scr[...] = h[tb - 1:tb]
    hlast_ref[...] = h[tb - 1:tb]
    o_ref[...] = (h * _gelu_tanh(yr_ref[...])).astype(o_ref.dtype)


def _lru(xr, yr, h0, buf0, conv_w, conv_b, wa, ba, wx, bx, lam, *, batch):
    m, w = xr.shape
    t_len = m // batch
    tb = min(256, t_len)
    nt = t_len // tb
    assert t_len % tb == 0 and tb % V7X_SUBLANES == 0
    tail = CONV_WIDTH - 1
    xr3, yr3 = xr.reshape(batch, t_len, w), yr.reshape(batch, t_len, w)
    rows = pl.BlockSpec((None, tb, w), lambda b, t: (b, t, 0))
    vec = pl.BlockSpec((1, w), lambda b, t: (0, 0))
    wspec = pl.BlockSpec(wa.shape, lambda b, t: (0, 0, 0))
    out, h_last, conv_tail = pl.pallas_call(
        functools.partial(_lru_kernel, tb=tb),
        grid=(batch, nt),
        in_specs=[rows, rows,
                  pl.BlockSpec((None, 1, w), lambda b, t: (b, 0, 0)),
                  pl.BlockSpec((None, tail, w), lambda b, t: (b, 0, 0)),
                  pl.BlockSpec((CONV_WIDTH, w), lambda b, t: (0, 0)), vec, wspec, vec, wspec, vec, vec],
        out_specs=[rows,
                   pl.BlockSpec((None, 1, w), lambda b, t: (b, 0, 0)),
                   pl.BlockSpec((None, tail, w), lambda b, t: (b, 0, 0))],
        out_shape=[jax.ShapeDtypeStruct((batch, t_len, w), BF16),
                   jax.ShapeDtypeStruct((batch, 1, w), F32),
                   jax.ShapeDtypeStruct((batch, tail, w), F32)],
        scratch_shapes=[pltpu.VMEM((V7X_SUBLANES + tb, w), F32), pltpu.VMEM((1, w), F32)],
        compiler_params=_params(("arbitrary", "arbitrary"), 48),
    )(xr3, yr3, h0.reshape(batch, 1, w), buf0, conv_w, conv_b, wa, ba, wx, bx, lam)
    return out.reshape(m, w), h_last.reshape(batch, w), conv_tail


def _block_diag(w, per):
    n, d, e = w.shape
    wg = w.reshape(n // per, per, d, e)
    eye = jnp.eye(per, dtype=w.dtype)
    return jnp.einsum('gpde,pq->gpdqe', wg, eye).reshape(n // per, per * d, per * e)


def _merge_kernel(x_ref, attn_ref, lru_ref, gla_ref, glr_ref, bga_ref, bgr_ref, g_ref, wpa_ref, wpr_ref, wo_ref,
                  o_ref):
    pa = jnp.dot(attn_ref[...], wpa_ref[...], preferred_element_type=F32)
    pr = jnp.dot(lru_ref[...], wpr_ref[...], preferred_element_type=F32)
    merged = _sigmoid(gla_ref[...] + bga_ref[...]) * pa + _sigmoid(glr_ref[...] + bgr_ref[...]) * pr
    o_ref[...] = x_ref[...] + g_ref[...] * jnp.dot(merged.astype(BF16), wo_ref[...], preferred_element_type=F32)


def _merge(x, attn, lru_out, gl, b_gate, g, w_pa, w_pr, w_o):
    m, d = x.shape
    tm = min(512, m)
    assert m % tm == 0
    row = pl.BlockSpec((tm, d), lambda i: (i, 0))
    wspec = _resident((d, d), lambda i: (0, 0))
    return pl.pallas_call(
        _merge_kernel,
        grid=(m // tm,),
        in_specs=[row, row, row, row, pl.BlockSpec((tm, d), lambda i: (i, 1)),
                  pl.BlockSpec((1, d), lambda i: (0, 0)), pl.BlockSpec((1, d), lambda i: (0, 1)),
                  _mod_spec(g, tm), wspec, wspec, wspec],
        out_specs=row,
        out_shape=jax.ShapeDtypeStruct((m, d), F32),
        compiler_params=_params(("arbitrary",), 48),
    )(x, attn, lru_out, gl, gl, b_gate, b_gate, g, w_pa, w_pr, w_o)


def _run_trunk(x, mods, p, *, batch, cache=None):
    depth = p['w_in'].shape[0]
    d = x.shape[1]
    ks, vs, hs, convs = [], [], [], []
    for l in range(depth):
        sh1, sc1, g1, sh2, sc2, g2, sh3, sc3, g3 = mods[l]
        lam_init = 0.8 - 0.6 * math.exp(-0.3 * l)
        lamp = p['lamp'][l]
        subg = p['attn_sub_g'][l][None, :]
        x = _ffn(x, sh1, sc1, g1, p['ln_ffn1'][l][None], p['ln_final'][None], p['ffn1_gu'][l], p['ffn1_down'][l],
                 final_norm=False)
        q, k, v, xr, yr, gl = _mixin(x, sh2, sc2, p['ln_mix'][l][None], p['w_in'][l])
        if cache is None:
            attn = _prompt_attention(q, k, v, p['rel_bias'], lamp, subg, lam_init)
            h0 = jnp.zeros((batch, d), F32)
            buf0 = jnp.zeros((batch, CONV_WIDTH - 1, d), F32)
        else:
            cache_k, cache_v, page_table, state_h, state_conv = cache
            attn = _decode_attention(q, k, v, cache_k, cache_v, l, page_table, p['rel_bias'], lamp, subg, lam_init)
            h0, buf0 = state_h[l], state_conv[l]
        lru_out, h_last, conv_tail = _lru(xr, yr, h0, buf0, p['conv_w'][l], p['conv_b'][l][None], p['rg_wa'][l],
                                          p['rg_ba'][l][None], p['rg_wx'][l], p['rg_bx'][l][None],
                                          p['rg_lambda'][l][None], batch=batch)
        x = _merge(x, attn, lru_out, gl, p['b_gate'][l][None], g2, p['w_pa'][l], p['w_pr'][l], p['w_o'][l])
        x = _ffn(x, sh3, sc3, g3, p['ln_ffn2'][l][None], p['ln_final'][None], p['ffn2_gu'][l], p['ffn2_down'][l],
                 final_norm=(l == depth - 1))
        t_len = x.shape[0] // batch
        ks.append(k.reshape(batch, t_len, N_KV_HEADS, 2 * HEAD_DIM))
        vs.append(v.reshape(batch, t_len, N_KV_HEADS, V_DIM))
        hs.append(h_last)
        convs.append(conv_tail)
    return x, jnp.stack(ks), jnp.stack(vs), jnp.stack(hs), jnp.stack(convs)


def kernel(x_prompt, x_sample, cache_k, cache_v, state_h, state_conv, page_table, c_prompt, c_sample, rel_bias, ada_w, ada_b, ln_ffn1, ffn1_gu, ffn1_down, ln_mix, w_in, b_gate, lambda_q1, lambda_k1, lambda_q2, lambda_k2, attn_sub_g, conv_w, conv_b, rg_wa, rg_ba, rg_wx, rg_bx, rg_lambda, w_pa, w_pr, w_o, ln_ffn2, ffn2_gu, ffn2_down, ln_final):
    b_p, s_len, d = x_prompt.shape
    b_d, t_dec, _ = x_sample.shape
    depth = w_in.shape[0]
    per = V7X_MXU_DIM // (rg_wa.shape[-1])
    p = dict(
        rel_bias=rel_bias, ln_ffn1=ln_ffn1, ln_mix=ln_mix, ln_ffn2=ln_ffn2, ln_final=ln_final, b_gate=b_gate,
        attn_sub_g=attn_sub_g, conv_w=conv_w, conv_b=conv_b, rg_ba=rg_ba, rg_bx=rg_bx, rg_lambda=rg_lambda,
        ffn1_gu=ffn1_gu.astype(BF16), ffn1_down=ffn1_down.astype(BF16), w_in=w_in.astype(BF16),
        w_pa=w_pa.astype(BF16), w_pr=w_pr.astype(BF16), w_o=w_o.astype(BF16),
        ffn2_gu=ffn2_gu.astype(BF16), ffn2_down=ffn2_down.astype(BF16),
        rg_wa=jnp.stack([_block_diag(rg_wa[l].astype(BF16), per) for l in range(depth)]),
        rg_wx=jnp.stack([_block_diag(rg_wx[l].astype(BF16), per) for l in range(depth)]),
        lamp=jnp.stack([lambda_q1, lambda_k1, lambda_q2, lambda_k2], axis=1).astype(F32),
    )

    n_seq = b_p + b_d
    pad = (-n_seq) % V7X_SUBLANES
    c_all = jnp.pad(jnp.concatenate([c_prompt, c_sample], axis=0), ((0, pad), (0, 0)))
    mods = _ada_mods(c_all, ada_w, ada_b)

    def split(rows, repeat):
        out = []
        for l in range(depth):
            parts = [rows[l][:, k * d:(k + 1) * d] for k in range(N_MOD)]
            out.append([jnp.repeat(m, repeat, axis=0) if repeat > 1 else m for m in parts])
        return out

    assert b_p == 1
    mods_p = split(mods[:, :b_p], 1)
    mods_d = split(mods[:, b_p:n_seq], t_dec)

    y_p, k_p, v_p, h_p, conv_p = _run_trunk(x_prompt.reshape(b_p * s_len, d), mods_p, p, batch=b_p)
    y_d, k_d, v_d, h_d, conv_d = _run_trunk(x_sample.reshape(b_d * t_dec, d), mods_d, p, batch=b_d,
                                            cache=(cache_k, cache_v, page_table, state_h, state_conv))
    return (y_p.reshape(b_p, s_len, d), y_d.reshape(b_d, t_dec, d), k_p, v_p, h_p, conv_p, k_d, v_d, h_d, conv_d)
```

```python
import functools
import math

import jax
import jax.numpy as jnp
import numpy as np
from jax import lax
from jax.experimental import pallas as pl
from jax.experimental.pallas import tpu as pltpu

F32 = jnp.float32
BF16 = jnp.bfloat16

N_HEADS = 8
N_KV_HEADS = 4
GROUP = N_HEADS // N_KV_HEADS
HEAD_DIM = 64
V_DIM = 2 * HEAD_DIM
LRU_BLOCKS = 16
CONV_WIDTH = 4
LRU_C = 8.0
N_BUCKETS = 32
MAX_DISTANCE = 128
NORM_EPS = 1e-6
N_MOD = 9

V7X_LANES = 128
V7X_SUBLANES = 8
V7X_MXU_DIM = 256
V7X_VMEM_BYTES = 64 * 1024 * 1024
MIB = 1024 * 1024


def _params(semantics, vmem_mib):
    assert vmem_mib * MIB < V7X_VMEM_BYTES
    return pltpu.CompilerParams(dimension_semantics=semantics, vmem_limit_bytes=vmem_mib * MIB)


def _resident(shape, index_map):
    return pl.BlockSpec(shape, index_map, pipeline_mode=pl.Buffered(1))


def _sigmoid(x):
    return 1.0 / (1.0 + jnp.exp(-x))


def _modulated_norm(x, ln, sc, sh):
    y = x * lax.rsqrt(jnp.mean(x * x, axis=-1, keepdims=True) + NORM_EPS) * ln
    return y * (1.0 + sc) + sh


def _mod_spec(mod, tm):
    if mod.shape[0] == 1:
        return pl.BlockSpec((1, mod.shape[1]), lambda i: (0, 0))
    return pl.BlockSpec((tm, mod.shape[1]), lambda i: (i, 0))


def _ada_kernel(c_ref, w_ref, b_ref, o_ref):
    c = c_ref[...]
    act = c * _sigmoid(c)
    o_ref[...] = jnp.dot(act, w_ref[...], preferred_element_type=F32) + b_ref[...]


def _ada_mods(c_all, ada_w, ada_b):
    depth, d, n = ada_w.shape
    mp = c_all.shape[0]
    tn = n // 8
    assert tn % V7X_LANES == 0 and mp % V7X_SUBLANES == 0
    return pl.pallas_call(
        _ada_kernel,
        grid=(depth, n // tn),
        in_specs=[
            pl.BlockSpec((mp, d), lambda l, j: (0, 0)),
            pl.BlockSpec((None, d, tn), lambda l, j: (l, 0, j)),
            pl.BlockSpec((None, 1, tn), lambda l, j: (l, 0, j)),
        ],
        out_specs=pl.BlockSpec((None, mp, tn), lambda l, j: (l, 0, j)),
        out_shape=jax.ShapeDtypeStruct((depth, mp, n), F32),
        compiler_params=_params(("arbitrary", "arbitrary"), 24),
    )(c_all, ada_w, ada_b.reshape(depth, 1, n))


def _ffn_kernel(x_ref, sh_ref, sc_ref, g_ref, ln_ref, lnf_ref, wgu_ref, wd_ref, o_ref, *, d_ff, tf, final_norm):
    x = x_ref[...]
    h = _modulated_norm(x, ln_ref[...], sc_ref[...], sh_ref[...]).astype(BF16)
    acc = jnp.zeros(x.shape, F32)
    for c in range(d_ff // tf):
        gate = jnp.dot(h, wgu_ref[:, c * tf:(c + 1) * tf], preferred_element_type=F32)
        up = jnp.dot(h, wgu_ref[:, d_ff + c * tf:d_ff + (c + 1) * tf], preferred_element_type=F32)
        act = (gate * _sigmoid(gate) * up).astype(BF16)
        acc = acc + jnp.dot(act, wd_ref[c * tf:(c + 1) * tf, :], preferred_element_type=F32)
    y = x + 0.5 * g_ref[...] * acc
    if final_norm:
        y = y * lax.rsqrt(jnp.mean(y * y, axis=-1, keepdims=True) + NORM_EPS) * lnf_ref[...]
    o_ref[...] = y


def _ffn(x, sh, sc, g, ln, lnf, w_gu, w_down, *, final_norm):
    m, d = x.shape
    d_ff = w_down.shape[0]
    tm = min(512, m)
    tf = V7X_MXU_DIM
    assert m % tm == 0 and d_ff % tf == 0
    row = pl.BlockSpec((tm, d), lambda i: (i, 0))
    vec = pl.BlockSpec((1, d), lambda i: (0, 0))
    return pl.pallas_call(
        functools.partial(_ffn_kernel, d_ff=d_ff, tf=tf, final_norm=final_norm),
        grid=(m // tm,),
        in_specs=[row, _mod_spec(sh, tm), _mod_spec(sc, tm), _mod_spec(g, tm), vec, vec,
                  _resident(w_gu.shape, lambda i: (0, 0)), _resident(w_down.shape, lambda i: (0, 0))],
        out_specs=row,
        out_shape=jax.ShapeDtypeStruct((m, d), F32),
        compiler_params=_params(("arbitrary",), 48),
    )(x, sh, sc, g, ln, lnf, w_gu, w_down)


def _mixin_kernel(x_ref, sh_ref, sc_ref, ln_ref, w_ref, q_ref, k_ref, v_ref, xr_ref, yr_ref, gl_ref):
    h = _modulated_norm(x_ref[...], ln_ref[...], sc_ref[...], sh_ref[...]).astype(BF16)
    col = 0
    for ref, scale in ((q_ref, HEAD_DIM ** -0.5), (k_ref, None), (v_ref, None), (xr_ref, None), (yr_ref, None),
                       (gl_ref, None)):
        width = ref.shape[1]
        y = jnp.dot(h, w_ref[:, col:col + width], preferred_element_type=F32)
        if scale is not None:
            y = y * scale
        ref[...] = y.astype(ref.dtype)
        col += width


def _mixin(x, sh, sc, ln, w_in):
    m, d = x.shape
    q_cols, kv_cols = N_HEADS * 2 * HEAD_DIM, N_KV_HEADS * 2 * HEAD_DIM
    tm = min(256, m)
    assert m % tm == 0 and w_in.shape[1] == q_cols + 2 * kv_cols + 4 * d
    widths = (q_cols, kv_cols, kv_cols, d, d, 2 * d)
    dtypes = (BF16, F32, F32, F32, F32, F32)
    return pl.pallas_call(
        _mixin_kernel,
        grid=(m // tm,),
        in_specs=[pl.BlockSpec((tm, d), lambda i: (i, 0)), _mod_spec(sh, tm), _mod_spec(sc, tm),
                  pl.BlockSpec((1, d), lambda i: (0, 0)), _resident(w_in.shape, lambda i: (0, 0))],
        out_specs=[pl.BlockSpec((tm, w), lambda i: (i, 0)) for w in widths],
        out_shape=[jax.ShapeDtypeStruct((m, w), dt) for w, dt in zip(widths, dtypes)],
        compiler_params=_params(("arbitrary",), 48),
    )(x, sh, sc, ln, w_in)


def _rel_bucket(n):
    max_exact = N_BUCKETS // 2
    nf = jnp.maximum(n, 1).astype(F32)
    large = max_exact + (jnp.log(nf / max_exact) / math.log(MAX_DISTANCE / max_exact)
                         * (N_BUCKETS - max_exact)).astype(jnp.int32)
    large = jnp.minimum(large, N_BUCKETS - 1)
    return jnp.where(n < max_exact, n, large)


def _bias_of_distance(rel_bias, n):
    table = rel_bias.astype(F32) - rel_bias[N_BUCKETS - 1].astype(F32)
    onehot = (_rel_bucket(jnp.maximum(n, 0))[..., None] == jnp.arange(N_BUCKETS, dtype=jnp.int32)).astype(F32)
    b = jnp.einsum('...b,bh->h...', onehot, table, precision=lax.Precision.HIGHEST)
    return jnp.where(n >= 0, b, -jnp.inf)


def _lambda(lamp_ref, lam_init):
    lp = lamp_ref[...]
    return jnp.exp(jnp.sum(lp[0:1] * lp[1:2])) - jnp.exp(jnp.sum(lp[2:3] * lp[3:4])) + lam_init


M_INIT = -1e30


ACC_ROWS = V_DIM + 16
LONG_RUN = 8
SHORT_RUN = 4


def _col_max(s):
    rows = s.shape[0]
    while rows > V7X_SUBLANES:
        rows //= 2
        s = jnp.maximum(s[:rows], s[rows:])
    return jnp.max(s, axis=0, keepdims=True)


def _prompt_attn_kernel(qt_ref, k_ref, vt_ref, bias_ref, lamp_ref, subg_ref, o_ref,
                        qz_scr, sa_scr, sb_scr, m_scr, acc_scr, *, tq, tk, lam_init):
    qi = pl.program_id(1)
    ncomb = 2 * GROUP
    ratio = tq // tk
    row = lax.broadcasted_iota(jnp.int32, (V_DIM, tq), 0)
    for g in range(GROUP):
        qg = qt_ref[g * V_DIM:(g + 1) * V_DIM, :]
        qz_scr[:, (2 * g) * tq:(2 * g + 1) * tq] = jnp.where(row < HEAD_DIM, qg, jnp.zeros_like(qg))
        qz_scr[:, (2 * g + 1) * tq:(2 * g + 2) * tq] = jnp.where(row >= HEAD_DIM, qg, jnp.zeros_like(qg))
    m_scr[...] = jnp.full(m_scr.shape, M_INIT, F32)
    acc_scr[...] = jnp.zeros(acc_scr.shape, F32)
    s_bufs = (sa_scr, sb_scr)

    def logits(j, dst):
        kb = k_ref[pl.ds(pl.multiple_of(j * tk, tk), tk), :]
        dst[...] = jnp.dot(kb, qz_scr[...], preferred_element_type=F32)

    def update(j, src, bias_row):
        vt = vt_ref[j]
        for c in range(ncomb):
            s = src[:, c * tq:(c + 1) * tq]
            if bias_row is not None:
                g = c // 2
                s = s + bias_ref[bias_row:bias_row + tk, g * tq:(g + 1) * tq]
            m_old = m_scr[c]
            m_new = jnp.maximum(m_old, _col_max(s))
            alpha = jnp.exp(m_old - m_new)
            p = jnp.exp(s - m_new).astype(BF16)
            acc_scr[c] = alpha * acc_scr[c] + jnp.dot(vt, p, preferred_element_type=F32)
            m_scr[c] = m_new

    def run(j, count):
        for u in range(count):
            logits(j + u + 1, s_bufs[(u + 1) % 2])
            update(j + u, s_bufs[u % 2], None)

    def by_parity(j, fn):
        for parity in range(2):
            @pl.when(j % 2 == parity)
            def _():
                fn(s_bufs[parity], s_bufs[1 - parity])

    logits(0, sa_scr)

    first = ratio * qi - 1
    n_far = jnp.maximum(first, 0)
    n_long = n_far // LONG_RUN

    def long_runs(t, carry):
        run(t * LONG_RUN, LONG_RUN)
        return carry

    lax.fori_loop(0, n_long, long_runs, 0)
    done = n_long * LONG_RUN
    short = n_far - done >= SHORT_RUN

    @pl.when(short)
    def _():
        run(done, SHORT_RUN)

    done = done + jnp.where(short, SHORT_RUN, 0)

    def single(j, carry):
        def body(cur, nxt):
            logits(j + 1, nxt)
            update(j, cur, None)
        by_parity(j, body)
        return carry

    lax.fori_loop(done, n_far, single, 0)

    def tail_with_near(cur, nxt):
        bufs = (cur, nxt)
        for t in range(ratio + 1):
            if t < ratio:
                logits(first + t + 1, bufs[(t + 1) % 2])
            update(first + t, bufs[t % 2], t * tk)

    def tail_without_near(cur, nxt):
        bufs = (cur, nxt)
        for t in range(ratio):
            if t + 1 < ratio:
                logits(t + 1, bufs[(t + 1) % 2])
            update(t, bufs[t % 2], (t + 1) * tk)

    @pl.when(qi > 0)
    def _():
        by_parity(first, tail_with_near)

    @pl.when(qi == 0)
    def _():
        tail_without_near(sa_scr, sb_scr)

    lam = _lambda(lamp_ref, lam_init)
    for g in range(GROUP):
        a0, a1 = acc_scr[2 * g], acc_scr[2 * g + 1]
        o = a0[:V_DIM] * (1.0 / a0[V_DIM:V_DIM + 1]) - lam * (a1[:V_DIM] * (1.0 / a1[V_DIM:V_DIM + 1]))
        o = o * lax.rsqrt(jnp.mean(o * o, axis=0, keepdims=True) + NORM_EPS) * subg_ref[...] * (1.0 - lam_init)
        o_ref[:, g * V_DIM:(g + 1) * V_DIM] = o.T.astype(o_ref.dtype)


def _prompt_attention(q, k, v, rel_bias, lamp, subg, lam_init):
    s_len = q.shape[0]
    tq = tk = min(V7X_MXU_DIM, s_len)
    nq, nk = s_len // tq, s_len // tk
    ratio = tq // tk
    assert s_len % tq == 0 and tk >= MAX_DISTANCE
    qt = q.T
    kb = k.astype(BF16)
    vt = v.astype(BF16).reshape(nk, tk, N_KV_HEADS, V_DIM).transpose(2, 0, 3, 1)
    ones = jnp.zeros((N_KV_HEADS, nk, ACC_ROWS - V_DIM, tk), BF16).at[:, :, 0, :].set(1.0)
    vt = jnp.concatenate([vt, ones], axis=2)
    nb = (ratio + 1) * tk
    c = jnp.arange(nb, dtype=jnp.int32)[:, None]
    r = jnp.arange(tq, dtype=jnp.int32)[None, :]
    bias = _bias_of_distance(rel_bias, tk + r - c)
    bias = bias.reshape(N_KV_HEADS, GROUP, nb, tq).transpose(0, 2, 1, 3).reshape(N_KV_HEADS, nb, GROUP * tq)
    subg_col = subg.reshape(V_DIM, 1)
    ncomb = 2 * GROUP
    return pl.pallas_call(
        functools.partial(_prompt_attn_kernel, tq=tq, tk=tk, lam_init=lam_init),
        grid=(N_KV_HEADS, nq),
        in_specs=[
            pl.BlockSpec((GROUP * V_DIM, tq), lambda h, i: (h, i)),
            pl.BlockSpec((s_len, V_DIM), lambda h, i: (0, h)),
            pl.BlockSpec((None, nk, ACC_ROWS, tk), lambda h, i: (h, 0, 0, 0)),
            pl.BlockSpec((None, nb, GROUP * tq), lambda h, i: (h, 0, 0)),
            pl.BlockSpec(lamp.shape, lambda h, i: (0, 0)),
            pl.BlockSpec(subg_col.shape, lambda h, i: (0, 0)),
        ],
        out_specs=pl.BlockSpec((tq, GROUP * V_DIM), lambda h, i: (i, h)),
        out_shape=jax.ShapeDtypeStruct((s_len, N_HEADS * V_DIM), BF16),
        scratch_shapes=[pltpu.VMEM((V_DIM, ncomb * tq), BF16),
                        pltpu.VMEM((tk, ncomb * tq), F32), pltpu.VMEM((tk, ncomb * tq), F32),
                        pltpu.VMEM((ncomb, 1, tq), F32), pltpu.VMEM((ncomb, ACC_ROWS, tq), F32)],
        compiler_params=_params(("arbitrary", "arbitrary"), 40),
    )(qt, kb, vt, bias, lamp, subg_col)


def _decode_softmax_step(s, v_heads, m_ref, l_ref, acc_ref):
    m_old = m_ref[...]
    m_new = jnp.maximum(m_old, jnp.max(s, axis=-1, keepdims=True))
    alpha = jnp.exp(m_old - m_new)
    p = jnp.exp(s - m_new)
    l_ref[...] = alpha * l_ref[...] + jnp.sum(p, axis=-1, keepdims=True)
    pb = p.astype(BF16)
    nk = pb.shape[1] // len(v_heads)
    rows = pb.shape[0] // N_KV_HEADS
    parts = []
    for h in range(N_KV_HEADS):
        pv = None
        for i, vh in enumerate(v_heads):
            part = jnp.dot(pb[h * rows:(h + 1) * rows, i * nk:(i + 1) * nk], vh(h), preferred_element_type=F32)
            pv = part if pv is None else pv + part
        parts.append(pv)
    acc_ref[...] = alpha * acc_ref[...] + jnp.concatenate(parts, axis=0)
    m_ref[...] = m_new


def _decode_attn_kernel(pt_ref, q_ref, bias_ref, knew_ref, vnew_ref, bnew_ref, lamp_ref, subg_ref, *rest,
                        pages, page, lam_init):
    del pt_ref
    k_refs, v_refs = rest[:pages], rest[pages:2 * pages]
    o_ref, m_scr, l_scr, acc_scr = rest[2 * pages:]
    step = pl.program_id(1)

    @pl.when(step == 0)
    def _():
        m_scr[...] = jnp.full(m_scr.shape, M_INIT, F32)
        l_scr[...] = jnp.zeros(l_scr.shape, F32)
        acc_scr[...] = jnp.zeros(acc_scr.shape, F32)

    q = q_ref[...]
    rows = q.shape[0] // N_KV_HEADS
    nt = (((1,), (1,)), ((), ()))

    def head(ref, h):
        return ref[pl.ds(h, page, stride=N_KV_HEADS), :].astype(BF16)

    def logits(ref):
        return jnp.concatenate(
            [lax.dot_general(q[h * rows:(h + 1) * rows], head(ref, h), nt, preferred_element_type=F32)
             for h in range(N_KV_HEADS)], axis=0)

    s = jnp.concatenate([logits(k_refs[i]) for i in range(pages)], axis=1) + bias_ref[...]
    _decode_softmax_step(s, [functools.partial(head, v_refs[i]) for i in range(pages)], m_scr, l_scr, acc_scr)

    @pl.when(step == pl.num_programs(1) - 1)
    def _():
        s_new = logits(knew_ref) + bnew_ref[...]
        _decode_softmax_step(s_new, [functools.partial(head, vnew_ref)], m_scr, l_scr, acc_scr)
        lam = _lambda(lamp_ref, lam_init)
        out = acc_scr[...] / l_scr[...]
        for h in range(N_KV_HEADS):
            blk = out[h * rows:(h + 1) * rows]
            o = blk[:rows // 2] - lam * blk[rows // 2:]
            o = o * lax.rsqrt(jnp.mean(o * o, axis=-1, keepdims=True) + NORM_EPS) * subg_ref[...] * (1.0 - lam_init)
            o_ref[h * (rows // 2):(h + 1) * (rows // 2), :] = o.astype(o_ref.dtype)


def _decode_attention(q, k_new, v_new, cache_k, cache_v, layer, page_table, rel_bias, lamp, subg, lam_init):
    batch, n_pages = page_table.shape
    depth, n_pool, page, kvh, kdim = cache_k.shape
    t_len = q.shape[0] // batch
    rows = N_KV_HEADS * 2 * GROUP * t_len
    pages = 16
    assert n_pages % pages == 0 and page >= MAX_DISTANCE and t_len <= page
    assert kvh == N_KV_HEADS and kdim == V_DIM
    n_steps = n_pages // pages
    prow = page * kvh
    ck = cache_k.reshape(depth, n_pool, prow, kdim)
    cv = cache_v.reshape(depth, n_pool, prow, kdim)

    q6 = q.reshape(batch, t_len, N_KV_HEADS, GROUP, 2, HEAD_DIM).transpose(0, 2, 4, 3, 1, 5)
    eye_m = jnp.eye(2, dtype=q.dtype)[None, None, :, None, None, :, None]
    q_all = (q6[:, :, :, :, :, None, :] * eye_m).reshape(batch, rows, 2 * HEAD_DIM)

    row_h = (jnp.arange(N_KV_HEADS)[:, None, None, None] * GROUP + jnp.arange(GROUP)[None, None, :, None]
             + jnp.zeros((1, 2, 1, t_len), jnp.int32)).reshape(rows)
    row_t = (jnp.zeros((N_KV_HEADS, 2, GROUP, 1), jnp.int32) + jnp.arange(t_len, dtype=jnp.int32)).reshape(rows)
    key = jnp.arange(page, dtype=jnp.int32)[None, :]
    sel = jax.nn.one_hot(row_h, N_HEADS, dtype=F32)
    last = jnp.einsum('hck,ch->ck', _bias_of_distance(rel_bias, page + row_t[:, None] - key), sel)
    bias_steps = jnp.zeros((2, rows, pages * page), F32).at[1, :, (pages - 1) * page:].set(last)
    n_new = jnp.where(key < t_len, row_t[:, None] - key, -1)
    b_all = _bias_of_distance(rel_bias, n_new)
    b_new = jnp.sum(jnp.where(sel.T[:, :, None] > 0, b_all, 0.0), axis=0)
    pad = ((0, 0), (0, page - t_len), (0, 0))
    k_pad = jnp.pad(k_new.reshape(batch, t_len, kvh * kdim), pad).reshape(batch, prow, kdim)
    v_pad = jnp.pad(v_new.reshape(batch, t_len, kvh * kdim), pad).reshape(batch, prow, kdim)

    def page_spec(i):
        return pl.BlockSpec((None, None, prow, kdim), lambda b, s, pt: (layer, pt[b, s * pages + i], 0, 0))

    grid_spec = pltpu.PrefetchScalarGridSpec(
        num_scalar_prefetch=1,
        grid=(batch, n_steps),
        in_specs=[
            pl.BlockSpec((None, rows, 2 * HEAD_DIM), lambda b, s, pt: (b, 0, 0)),
            pl.BlockSpec((None, rows, pages * page), lambda b, s, pt: (jnp.where(s == n_steps - 1, 1, 0), 0, 0)),
            pl.BlockSpec((None, prow, kdim), lambda b, s, pt: (b, 0, 0)),
            pl.BlockSpec((None, prow, kdim), lambda b, s, pt: (b, 0, 0)),
            pl.BlockSpec((rows, page), lambda b, s, pt: (0, 0)),
            pl.BlockSpec(lamp.shape, lambda b, s, pt: (0, 0)),
            pl.BlockSpec(subg.shape, lambda b, s, pt: (0, 0)),
        ] + [page_spec(i) for i in range(pages)] * 2,
        out_specs=pl.BlockSpec((None, rows // 2, V_DIM), lambda b, s, pt: (b, 0, 0)),
        scratch_shapes=[pltpu.VMEM((rows, 1), F32), pltpu.VMEM((rows, 1), F32), pltpu.VMEM((rows, V_DIM), F32)],
    )
    out = pl.pallas_call(
        functools.partial(_decode_attn_kernel, pages=pages, page=page, lam_init=lam_init),
        grid_spec=grid_spec,
        out_shape=jax.ShapeDtypeStruct((batch, rows // 2, V_DIM), BF16),
        compiler_params=_params(("arbitrary", "arbitrary"), 32),
    )(page_table, q_all, bias_steps, k_pad, v_pad, b_new, lamp, subg, *([ck] * pages), *([cv] * pages))
    out = out.reshape(batch, N_KV_HEADS, GROUP, t_len, V_DIM).transpose(0, 3, 1, 2, 4)
    return out.reshape(batch * t_len, N_HEADS * V_DIM)


def _gelu_tanh(x):
    return 0.5 * x * (1.0 + jnp.tanh(math.sqrt(2.0 / math.pi) * (x + 0.044715 * (x * x * x))))


def _lru_kernel(xr_ref, yr_ref, h0_ref, buf0_ref, cw_ref, cb_ref, wa_ref, ba_ref, wx_ref, bx_ref, lam_ref,
                o_ref, hlast_ref, tail_ref, xp_scr, h_scr, *, tb):
    pad = V7X_SUBLANES
    tail = CONV_WIDTH - 1

    @pl.when(pl.program_id(1) == 0)
    def _():
        h_scr[...] = h0_ref[...]
        xp_scr[pad - tail:pad, :] = buf0_ref[...]

    x = xr_ref[...]
    xp_scr[pad:pad + tb, :] = x
    cw = cw_ref[...]
    xc = cb_ref[...] + xp_scr[pad - tail:pad - tail + tb, :] * cw[0:1]
    for j in range(1, CONV_WIDTH):
        xc = xc + xp_scr[pad - tail + j:pad - tail + j + tb, :] * cw[j:j + 1]
    new_tail = xp_scr[pad + tb - tail:pad + tb, :]
    tail_ref[...] = new_tail
    xp_scr[pad - tail:pad, :] = new_tail

    xcb = xc.astype(BF16)
    nblk, bw = wa_ref.shape[0], wa_ref.shape[1]

    def gate(w_ref, b_ref):
        parts = [jnp.dot(xcb[:, c * bw:(c + 1) * bw], w_ref[c], preferred_element_type=F32) for c in range(nblk)]
        return _sigmoid(jnp.concatenate(parts, axis=1) + b_ref[...])

    r = gate(wa_ref, ba_ref)
    i = gate(wx_ref, bx_ref)
    neg_lam = -lam_ref[...]
    softplus = jnp.maximum(neg_lam, 0.0) + jnp.log(1.0 + jnp.exp(-jnp.abs(neg_lam)))
    log_a = -LRU_C * r * softplus
    a = jnp.exp(log_a)
    y = -jnp.tanh(log_a) * (a * a + 1.0)
    u = jnp.where(y > 0.0, y * lax.rsqrt(y), 0.0) * i * xc

    sub = V7X_SUBLANES
    nt = tb // sub
    big_a = a.reshape(nt, sub, a.shape[1])
    big_b = u.reshape(nt, sub, a.shape[1])
    row = lax.broadcasted_iota(jnp.int32, big_a.shape, 1)
    shift = 1
    while shift < sub:
        keep = row >= shift
        a_prev = jnp.where(keep, pltpu.roll(big_a, shift, 1), 1.0)
        b_prev = jnp.where(keep, pltpu.roll(big_b, shift, 1), 0.0)
        big_b = big_a * b_prev + big_b
        big_a = big_a * a_prev
        shift *= 2
    carry = h_scr[...]
    tiles = []
    for t in range(nt):
        h_t = big_a[t] * carry + big_b[t]
        carry = h_t[sub - 1:sub]
        tiles.append(h_t)
    h = jnp.concatenate(tiles, axis=0)
    h_scr[...] = h[tb - 1:tb]
    hlast_ref[...] = h[tb - 1:tb]
    o_ref[...] = (h * _gelu_tanh(yr_ref[...])).astype(o_ref.dtype)


def _lru(xr, yr, h0, buf0, conv_w, conv_b, wa, ba, wx, bx, lam, *, batch):
    m, w = xr.shape
    t_len = m // batch
    tb = min(256, t_len)
    nt = t_len // tb
    assert t_len % tb == 0 and tb % V7X_SUBLANES == 0
    tail = CONV_WIDTH - 1
    xr3, yr3 = xr.reshape(batch, t_len, w), yr.reshape(batch, t_len, w)
    rows = pl.BlockSpec((None, tb, w), lambda b, t: (b, t, 0))
    vec = pl.BlockSpec((1, w), lambda b, t: (0, 0))
    wspec = pl.BlockSpec(wa.shape, lambda b, t: (0, 0, 0))
    out, h_last, conv_tail = pl.pallas_call(
        functools.partial(_lru_kernel, tb=tb),
        grid=(batch, nt),
        in_specs=[rows, rows,
                  pl.BlockSpec((None, 1, w), lambda b, t: (b, 0, 0)),
                  pl.BlockSpec((None, tail, w), lambda b, t: (b, 0, 0)),
                  pl.BlockSpec((CONV_WIDTH, w), lambda b, t: (0, 0)), vec, wspec, vec, wspec, vec, vec],
        out_specs=[rows,
                   pl.BlockSpec((None, 1, w), lambda b, t: (b, 0, 0)),
                   pl.BlockSpec((None, tail, w), lambda b, t: (b, 0, 0))],
        out_shape=[jax.ShapeDtypeStruct((batch, t_len, w), BF16),
                   jax.ShapeDtypeStruct((batch, 1, w), F32),
                   jax.ShapeDtypeStruct((batch, tail, w), F32)],
        scratch_shapes=[pltpu.VMEM((V7X_SUBLANES + tb, w), F32), pltpu.VMEM((1, w), F32)],
        compiler_params=_params(("arbitrary", "arbitrary"), 48),
    )(xr3, yr3, h0.reshape(batch, 1, w), buf0, conv_w, conv_b, wa, ba, wx, bx, lam)
    return out.reshape(m, w), h_last.reshape(batch, w), conv_tail


def _block_diag(w, per):
    n, d, e = w.shape
    wg = w.reshape(n // per, per, d, e)
    eye = jnp.eye(per, dtype=w.dtype)
    return jnp.einsum('gpde,pq->gpdqe', wg, eye).reshape(n // per, per * d, per * e)


def _merge_kernel(x_ref, attn_ref, lru_ref, gla_ref, glr_ref, bga_ref, bgr_ref, g_ref, wpa_ref, wpr_ref, wo_ref,
                  o_ref):
    pa = jnp.dot(attn_ref[...], wpa_ref[...], preferred_element_type=F32)
    pr = jnp.dot(lru_ref[...], wpr_ref[...], preferred_element_type=F32)
    merged = _sigmoid(gla_ref[...] + bga_ref[...]) * pa + _sigmoid(glr_ref[...] + bgr_ref[...]) * pr
    o_ref[...] = x_ref[...] + g_ref[...] * jnp.dot(merged.astype(BF16), wo_ref[...], preferred_element_type=F32)


def _merge(x, attn, lru_out, gl, b_gate, g, w_pa, w_pr, w_o):
    m, d = x.shape
    tm = min(512, m)
    assert m % tm == 0
    row = pl.BlockSpec((tm, d), lambda i: (i, 0))
    wspec = _resident((d, d), lambda i: (0, 0))
    return pl.pallas_call(
        _merge_kernel,
        grid=(m // tm,),
        in_specs=[row, row, row, row, pl.BlockSpec((tm, d), lambda i: (i, 1)),
                  pl.BlockSpec((1, d), lambda i: (0, 0)), pl.BlockSpec((1, d), lambda i: (0, 1)),
                  _mod_spec(g, tm), wspec, wspec, wspec],
        out_specs=row,
        out_shape=jax.ShapeDtypeStruct((m, d), F32),
        compiler_params=_params(("arbitrary",), 48),
    )(x, attn, lru_out, gl, gl, b_gate, b_gate, g, w_pa, w_pr, w_o)


def _run_trunk(x, mods, p, *, batch, cache=None):
    depth = p['w_in'].shape[0]
    d = x.shape[1]
    ks, vs, hs, convs = [], [], [], []
    for l in range(depth):
        sh1, sc1, g1, sh2, sc2, g2, sh3, sc3, g3 = mods[l]
        lam_init = 0.8 - 0.6 * math.exp(-0.3 * l)
        lamp = p['lamp'][l]
        subg = p['attn_sub_g'][l][None, :]
        x = _ffn(x, sh1, sc1, g1, p['ln_ffn1'][l][None], p['ln_final'][None], p['ffn1_gu'][l], p['ffn1_down'][l],
                 final_norm=False)
        q, k, v, xr, yr, gl = _mixin(x, sh2, sc2, p['ln_mix'][l][None], p['w_in'][l])
        if cache is None:
            attn = _prompt_attention(q, k, v, p['rel_bias'], lamp, subg, lam_init)
            h0 = jnp.zeros((batch, d), F32)
            buf0 = jnp.zeros((batch, CONV_WIDTH - 1, d), F32)
        else:
            cache_k, cache_v, page_table, state_h, state_conv = cache
            attn = _decode_attention(q, k, v, cache_k, cache_v, l, page_table, p['rel_bias'], lamp, subg, lam_init)
            h0, buf0 = state_h[l], state_conv[l]
        lru_out, h_last, conv_tail = _lru(xr, yr, h0, buf0, p['conv_w'][l], p['conv_b'][l][None], p['rg_wa'][l],
                                          p['rg_ba'][l][None], p['rg_wx'][l], p['rg_bx'][l][None],
                                          p['rg_lambda'][l][None], batch=batch)
        x = _merge(x, attn, lru_out, gl, p['b_gate'][l][None], g2, p['w_pa'][l], p['w_pr'][l], p['w_o'][l])
        x = _ffn(x, sh3, sc3, g3, p['ln_ffn2'][l][None], p['ln_final'][None], p['ffn2_gu'][l], p['ffn2_down'][l],
                 final_norm=(l == depth - 1))
        t_len = x.shape[0] // batch
        ks.append(k.reshape(batch, t_len, N_KV_HEADS, 2 * HEAD_DIM))
        vs.append(v.reshape(batch, t_len, N_KV_HEADS, V_DIM))
        hs.append(h_last)
        convs.append(conv_tail)
    return x, jnp.stack(ks), jnp.stack(vs), jnp.stack(hs), jnp.stack(convs)


def kernel(x_prompt, x_sample, cache_k, cache_v, state_h, state_conv, page_table, c_prompt, c_sample, rel_bias, ada_w, ada_b, ln_ffn1, ffn1_gu, ffn1_down, ln_mix, w_in, b_gate, lambda_q1, lambda_k1, lambda_q2, lambda_k2, attn_sub_g, conv_w, conv_b, rg_wa, rg_ba, rg_wx, rg_bx, rg_lambda, w_pa, w_pr, w_o, ln_ffn2, ffn2_gu, ffn2_down, ln_final):
    b_p, s_len, d = x_prompt.shape
    b_d, t_dec, _ = x_sample.shape
    depth = w_in.shape[0]
    per = V7X_MXU_DIM // (rg_wa.shape[-1])
    p = dict(
        rel_bias=rel_bias, ln_ffn1=ln_ffn1, ln_mix=ln_mix, ln_ffn2=ln_ffn2, ln_final=ln_final, b_gate=b_gate,
        attn_sub_g=attn_sub_g, conv_w=conv_w, conv_b=conv_b, rg_ba=rg_ba, rg_bx=rg_bx, rg_lambda=rg_lambda,
        ffn1_gu=ffn1_gu.astype(BF16), ffn1_down=ffn1_down.astype(BF16), w_in=w_in.astype(BF16),
        w_pa=w_pa.astype(BF16), w_pr=w_pr.astype(BF16), w_o=w_o.astype(BF16),
        ffn2_gu=ffn2_gu.astype(BF16), ffn2_down=ffn2_down.astype(BF16),
        rg_wa=jnp.stack([_block_diag(rg_wa[l].astype(BF16), per) for l in range(depth)]),
        rg_wx=jnp.stack([_block_diag(rg_wx[l].astype(BF16), per) for l in range(depth)]),
        lamp=jnp.stack([lambda_q1, lambda_k1, lambda_q2, lambda_k2], axis=1).astype(F32),
    )

    n_seq = b_p + b_d
    pad = (-n_seq) % V7X_SUBLANES
    c_all = jnp.pad(jnp.concatenate([c_prompt, c_sample], axis=0), ((0, pad), (0, 0)))
    mods = _ada_mods(c_all, ada_w, ada_b)

    def split(rows, repeat):
        out = []
        for l in range(depth):
            parts = [rows[l][:, k * d:(k + 1) * d] for k in range(N_MOD)]
            out.append([jnp.repeat(m, repeat, axis=0) if repeat > 1 else m for m in parts])
        return out

    assert b_p == 1
    mods_p = split(mods[:, :b_p], 1)
    mods_d = split(mods[:, b_p:n_seq], t_dec)

    y_p, k_p, v_p, h_p, conv_p = _run_trunk(x_prompt.reshape(b_p * s_len, d), mods_p, p, batch=b_p)
    y_d, k_d, v_d, h_d, conv_d = _run_trunk(x_sample.reshape(b_d * t_dec, d), mods_d, p, batch=b_d,
                                            cache=(cache_k, cache_v, page_table, state_h, state_conv))
    return (y_p.reshape(b_p, s_len, d), y_d.reshape(b_d, t_dec, d), k_p, v_p, h_p, conv_p, k_d, v_d, h_d, conv_d)
```

```python
import functools
import math

import jax
import jax.numpy as jnp
import numpy as np
from jax import lax
from jax.experimental import pallas as pl
from jax.experimental.pallas import tpu as pltpu

F32 = jnp.float32
BF16 = jnp.bfloat16

N_HEADS = 8
N_KV_HEADS = 4
GROUP = N_HEADS // N_KV_HEADS
HEAD_DIM = 64
V_DIM = 2 * HEAD_DIM
LRU_BLOCKS = 16
CONV_WIDTH = 4
LRU_C = 8.0
N_BUCKETS = 32
MAX_DISTANCE = 128
NORM_EPS = 1e-6
N_MOD = 9

V7X_LANES = 128
V7X_SUBLANES = 8
V7X_MXU_DIM = 256
V7X_VMEM_BYTES = 64 * 1024 * 1024
MIB = 1024 * 1024

ATTN_TILE = V7X_MXU_DIM


def _params(semantics, vmem_mib):
    assert vmem_mib * MIB < V7X_VMEM_BYTES
    return pltpu.CompilerParams(dimension_semantics=semantics, vmem_limit_bytes=vmem_mib * MIB)


def _resident(shape, index_map):
    return pl.BlockSpec(shape, index_map, pipeline_mode=pl.Buffered(1))


def _sigmoid(x):
    return 1.0 / (1.0 + jnp.exp(-x))


def _modulated_norm(x, ln, sc, sh):
    y = x * lax.rsqrt(jnp.mean(x * x, axis=-1, keepdims=True) + NORM_EPS) * ln
    return y * (1.0 + sc) + sh


def _mod_spec(mod, tm):
    if mod.shape[0] == 1:
        return pl.BlockSpec((1, mod.shape[1]), lambda i: (0, 0))
    return pl.BlockSpec((tm, mod.shape[1]), lambda i: (i, 0))


def _ada_kernel(c_ref, w_ref, b_ref, o_ref):
    c = c_ref[...]
    act = c * _sigmoid(c)
    o_ref[...] = jnp.dot(act, w_ref[...], preferred_element_type=F32) + b_ref[...]


def _ada_mods(c_all, ada_w, ada_b):
    depth, d, n = ada_w.shape
    mp = c_all.shape[0]
    tn = n // 8
    assert tn % V7X_LANES == 0 and mp % V7X_SUBLANES == 0
    return pl.pallas_call(
        _ada_kernel,
        grid=(depth, n // tn),
        in_specs=[
            pl.BlockSpec((mp, d), lambda l, j: (0, 0)),
            pl.BlockSpec((None, d, tn), lambda l, j: (l, 0, j)),
            pl.BlockSpec((None, 1, tn), lambda l, j: (l, 0, j)),
        ],
        out_specs=pl.BlockSpec((None, mp, tn), lambda l, j: (l, 0, j)),
        out_shape=jax.ShapeDtypeStruct((depth, mp, n), F32),
        compiler_params=_params(("arbitrary", "arbitrary"), 24),
    )(c_all, ada_w, ada_b.reshape(depth, 1, n))


def _ffn_kernel(x_ref, sh_ref, sc_ref, g_ref, ln_ref, lnf_ref, wgu_ref, wd_ref, o_ref, *, d_ff, tf, final_norm):
    x = x_ref[...]
    h = _modulated_norm(x, ln_ref[...], sc_ref[...], sh_ref[...]).astype(BF16)
    acc = jnp.zeros(x.shape, F32)
    for c in range(d_ff // tf):
        gate = jnp.dot(h, wgu_ref[:, c * tf:(c + 1) * tf], preferred_element_type=F32)
        up = jnp.dot(h, wgu_ref[:, d_ff + c * tf:d_ff + (c + 1) * tf], preferred_element_type=F32)
        act = (gate * _sigmoid(gate) * up).astype(BF16)
        acc = acc + jnp.dot(act, wd_ref[c * tf:(c + 1) * tf, :], preferred_element_type=F32)
    y = x + 0.5 * g_ref[...] * acc
    if final_norm:
        y = y * lax.rsqrt(jnp.mean(y * y, axis=-1, keepdims=True) + NORM_EPS) * lnf_ref[...]
    o_ref[...] = y


def _ffn(x, sh, sc, g, ln, lnf, w_gu, w_down, *, final_norm):
    m, d = x.shape
    d_ff = w_down.shape[0]
    tm = min(512, m)
    tf = V7X_MXU_DIM
    assert m % tm == 0 and d_ff % tf == 0
    row = pl.BlockSpec((tm, d), lambda i: (i, 0))
    vec = pl.BlockSpec((1, d), lambda i: (0, 0))
    return pl.pallas_call(
        functools.partial(_ffn_kernel, d_ff=d_ff, tf=tf, final_norm=final_norm),
        grid=(m // tm,),
        in_specs=[row, _mod_spec(sh, tm), _mod_spec(sc, tm), _mod_spec(g, tm), vec, vec,
                  _resident(w_gu.shape, lambda i: (0, 0)), _resident(w_down.shape, lambda i: (0, 0))],
        out_specs=row,
        out_shape=jax.ShapeDtypeStruct((m, d), F32),
        compiler_params=_params(("arbitrary",), 48),
    )(x, sh, sc, g, ln, lnf, w_gu, w_down)


def _mixin_kernel(x_ref, sh_ref, sc_ref, ln_ref, w_ref, *out_refs, prompt):
    h = _modulated_norm(x_ref[...], ln_ref[...], sc_ref[...], sh_ref[...]).astype(BF16)
    if prompt:
        qt_ref, k_ref, v_ref, kb_ref, vt_ref, xr_ref, yr_ref, gl_ref = out_refs
    else:
        q_ref, k_ref, v_ref, xr_ref, yr_ref, gl_ref = out_refs
    q_cols, kv_cols, d = N_HEADS * 2 * HEAD_DIM, k_ref.shape[1], xr_ref.shape[1]

    def proj(col, width):
        return jnp.dot(h, w_ref[:, col:col + width], preferred_element_type=F32)

    q = proj(0, q_cols) * HEAD_DIM ** -0.5
    k = proj(q_cols, kv_cols)
    v = proj(q_cols + kv_cols, kv_cols)
    k_ref[...] = k
    v_ref[...] = v
    if prompt:
        qt_ref[...] = q.T.astype(BF16)
        kb_ref[...] = k.astype(BF16)
        pad_row = lax.broadcasted_iota(jnp.int32, (ACC_ROWS - V_DIM, v.shape[0]), 0)
        ones = jnp.where(pad_row == 0, 1.0, 0.0).astype(BF16)
        for hh in range(N_KV_HEADS):
            vt_ref[hh, :V_DIM, :] = v[:, hh * V_DIM:(hh + 1) * V_DIM].T.astype(BF16)
            vt_ref[hh, V_DIM:, :] = ones
    else:
        q_ref[...] = q.astype(BF16)
    col = q_cols + 2 * kv_cols
    for ref in (xr_ref, yr_ref, gl_ref):
        ref[...] = proj(col, ref.shape[1])
        col += ref.shape[1]


def _mixin(x, sh, sc, ln, w_in, *, prompt):
    m, d = x.shape
    q_cols, kv_cols = N_HEADS * 2 * HEAD_DIM, N_KV_HEADS * 2 * HEAD_DIM
    tm = min(ATTN_TILE, m)
    assert m % tm == 0 and w_in.shape[1] == q_cols + 2 * kv_cols + 4 * d

    def rows(width, dtype):
        return pl.BlockSpec((tm, width), lambda i: (i, 0)), jax.ShapeDtypeStruct((m, width), dtype)

    if prompt:
        outs = [(pl.BlockSpec((q_cols, tm), lambda i: (0, i)), jax.ShapeDtypeStruct((q_cols, m), BF16)),
                rows(kv_cols, F32), rows(kv_cols, F32), rows(kv_cols, BF16),
                (pl.BlockSpec((N_KV_HEADS, None, ACC_ROWS, tm), lambda i: (0, i, 0, 0)),
                 jax.ShapeDtypeStruct((N_KV_HEADS, m // tm, ACC_ROWS, tm), BF16))]
    else:
        outs = [rows(q_cols, BF16), rows(kv_cols, F32), rows(kv_cols, F32)]
    outs += [rows(d, F32), rows(d, F32), rows(2 * d, F32)]
    return pl.pallas_call(
        functools.partial(_mixin_kernel, prompt=prompt),
        grid=(m // tm,),
        in_specs=[pl.BlockSpec((tm, d), lambda i: (i, 0)), _mod_spec(sh, tm), _mod_spec(sc, tm),
                  pl.BlockSpec((1, d), lambda i: (0, 0)), _resident(w_in.shape, lambda i: (0, 0))],
        out_specs=[o[0] for o in outs],
        out_shape=[o[1] for o in outs],
        compiler_params=_params(("arbitrary",), 48),
    )(x, sh, sc, ln, w_in)


def _rel_bucket(n):
    max_exact = N_BUCKETS // 2
    nf = jnp.maximum(n, 1).astype(F32)
    large = max_exact + (jnp.log(nf / max_exact) / math.log(MAX_DISTANCE / max_exact)
                         * (N_BUCKETS - max_exact)).astype(jnp.int32)
    large = jnp.minimum(large, N_BUCKETS - 1)
    return jnp.where(n < max_exact, n, large)


def _bias_of_distance(rel_bias, n):
    table = rel_bias.astype(F32) - rel_bias[N_BUCKETS - 1].astype(F32)
    onehot = (_rel_bucket(jnp.maximum(n, 0))[..., None] == jnp.arange(N_BUCKETS, dtype=jnp.int32)).astype(F32)
    b = jnp.einsum('...b,bh->h...', onehot, table, precision=lax.Precision.HIGHEST)
    return jnp.where(n >= 0, b, -jnp.inf)


def _lambda(lamp_ref, lam_init):
    lp = lamp_ref[...]
    return jnp.exp(jnp.sum(lp[0:1] * lp[1:2])) - jnp.exp(jnp.sum(lp[2:3] * lp[3:4])) + lam_init


M_INIT = -1e30


ACC_ROWS = V_DIM + 16
LONG_RUN = 8
SHORT_RUN = 4


def _col_max(s):
    rows = s.shape[0]
    while rows > V7X_SUBLANES:
        rows //= 2
        s = jnp.maximum(s[:rows], s[rows:])
    return jnp.max(s, axis=0, keepdims=True)


def _prompt_attn_kernel(qt_ref, k_ref, vt_ref, bias_ref, lamp_ref, subg_ref, o_ref,
                        qz_scr, sa_scr, sb_scr, m_scr, acc_scr, *, tq, tk, lam_init):
    qi = pl.program_id(1)
    ncomb = 2 * GROUP
    ratio = tq // tk
    row = lax.broadcasted_iota(jnp.int32, (V_DIM, tq), 0)
    for g in range(GROUP):
        qg = qt_ref[g * V_DIM:(g + 1) * V_DIM, :]
        qz_scr[:, (2 * g) * tq:(2 * g + 1) * tq] = jnp.where(row < HEAD_DIM, qg, jnp.zeros_like(qg))
        qz_scr[:, (2 * g + 1) * tq:(2 * g + 2) * tq] = jnp.where(row >= HEAD_DIM, qg, jnp.zeros_like(qg))
    m_scr[...] = jnp.full(m_scr.shape, M_INIT, F32)
    acc_scr[...] = jnp.zeros(acc_scr.shape, F32)
    s_bufs = (sa_scr, sb_scr)

    def logits(j, dst):
        kb = k_ref[pl.ds(pl.multiple_of(j * tk, tk), tk), :]
        dst[...] = jnp.dot(kb, qz_scr[...], preferred_element_type=F32)

    def update(j, src, bias_row):
        vt = vt_ref[j]
        for c in range(ncomb):
            s = src[:, c * tq:(c + 1) * tq]
            if bias_row is not None:
                g = c // 2
                s = s + bias_ref[bias_row:bias_row + tk, g * tq:(g + 1) * tq]
            m_old = m_scr[c]
            m_new = jnp.maximum(m_old, _col_max(s))
            alpha = jnp.exp(m_old - m_new)
            p = jnp.exp(s - m_new).astype(BF16)
            acc_scr[c] = alpha * acc_scr[c] + jnp.dot(vt, p, preferred_element_type=F32)
            m_scr[c] = m_new

    def run(j, count):
        for u in range(count):
            logits(j + u + 1, s_bufs[(u + 1) % 2])
            update(j + u, s_bufs[u % 2], None)

    def by_parity(j, fn):
        for parity in range(2):
            @pl.when(j % 2 == parity)
            def _():
                fn(s_bufs[parity], s_bufs[1 - parity])

    logits(0, sa_scr)

    first = ratio * qi - 1
    n_far = jnp.maximum(first, 0)
    n_long = n_far // LONG_RUN

    def long_runs(t, carry):
        run(t * LONG_RUN, LONG_RUN)
        return carry

    lax.fori_loop(0, n_long, long_runs, 0)
    done = n_long * LONG_RUN
    short = n_far - done >= SHORT_RUN

    @pl.when(short)
    def _():
        run(done, SHORT_RUN)

    done = done + jnp.where(short, SHORT_RUN, 0)

    def single(j, carry):
        def body(cur, nxt):
            logits(j + 1, nxt)
            update(j, cur, None)
        by_parity(j, body)
        return carry

    lax.fori_loop(done, n_far, single, 0)

    def tail_with_near(cur, nxt):
        bufs = (cur, nxt)
        for t in range(ratio + 1):
            if t < ratio:
                logits(first + t + 1, bufs[(t + 1) % 2])
            update(first + t, bufs[t % 2], t * tk)

    def tail_without_near(cur, nxt):
        bufs = (cur, nxt)
        for t in range(ratio):
            if t + 1 < ratio:
                logits(t + 1, bufs[(t + 1) % 2])
            update(t, bufs[t % 2], (t + 1) * tk)

    @pl.when(qi > 0)
    def _():
        by_parity(first, tail_with_near)

    @pl.when(qi == 0)
    def _():
        tail_without_near(sa_scr, sb_scr)

    lam = _lambda(lamp_ref, lam_init)
    for g in range(GROUP):
        a0, a1 = acc_scr[2 * g], acc_scr[2 * g + 1]
        o = a0[:V_DIM] * (1.0 / a0[V_DIM:V_DIM + 1]) - lam * (a1[:V_DIM] * (1.0 / a1[V_DIM:V_DIM + 1]))
        o = o * lax.rsqrt(jnp.mean(o * o, axis=0, keepdims=True) + NORM_EPS) * subg_ref[...] * (1.0 - lam_init)
        o_ref[:, g * V_DIM:(g + 1) * V_DIM] = o.T.astype(o_ref.dtype)


def _prompt_bias(rel_bias, tq, tk):
    nb = (tq // tk + 1) * tk
    c = jnp.arange(nb, dtype=jnp.int32)[:, None]
    r = jnp.arange(tq, dtype=jnp.int32)[None, :]
    bias = _bias_of_distance(rel_bias, tk + r - c)
    return bias.reshape(N_KV_HEADS, GROUP, nb, tq).transpose(0, 2, 1, 3).reshape(N_KV_HEADS, nb, GROUP * tq)


def _prompt_attention(qt, kb, vt, bias, lamp, subg, lam_init):
    s_len = qt.shape[1]
    nk, tk = vt.shape[1], vt.shape[3]
    tq = tk
    nq = s_len // tq
    ratio = tq // tk
    nb = bias.shape[1]
    assert s_len == nq * tq == nk * tk and tk >= MAX_DISTANCE and nb == (ratio + 1) * tk
    subg_col = subg.reshape(V_DIM, 1)
    ncomb = 2 * GROUP
    return pl.pallas_call(
        functools.partial(_prompt_attn_kernel, tq=tq, tk=tk, lam_init=lam_init),
        grid=(N_KV_HEADS, nq),
        in_specs=[
            pl.BlockSpec((GROUP * V_DIM, tq), lambda h, i: (h, i)),
            pl.BlockSpec((s_len, V_DIM), lambda h, i: (0, h)),
            pl.BlockSpec((None, nk, ACC_ROWS, tk), lambda h, i: (h, 0, 0, 0)),
            pl.BlockSpec((None, nb, GROUP * tq), lambda h, i: (h, 0, 0)),
            pl.BlockSpec(lamp.shape, lambda h, i: (0, 0)),
            pl.BlockSpec(subg_col.shape, lambda h, i: (0, 0)),
        ],
        out_specs=pl.BlockSpec((tq, GROUP * V_DIM), lambda h, i: (i, h)),
        out_shape=jax.ShapeDtypeStruct((s_len, N_HEADS * V_DIM), BF16),
        scratch_shapes=[pltpu.VMEM((V_DIM, ncomb * tq), BF16),
                        pltpu.VMEM((tk, ncomb * tq), F32), pltpu.VMEM((tk, ncomb * tq), F32),
                        pltpu.VMEM((ncomb, 1, tq), F32), pltpu.VMEM((ncomb, ACC_ROWS, tq), F32)],
        compiler_params=_params(("arbitrary", "arbitrary"), 40),
    )(qt, kb, vt, bias, lamp, subg_col)


def _decode_softmax_step(s, v_heads, m_ref, l_ref, acc_ref):
    m_old = m_ref[...]
    m_new = jnp.maximum(m_old, jnp.max(s, axis=-1, keepdims=True))
    alpha = jnp.exp(m_old - m_new)
    p = jnp.exp(s - m_new)
    l_ref[...] = alpha * l_ref[...] + jnp.sum(p, axis=-1, keepdims=True)
    pb = p.astype(BF16)
    nk = pb.shape[1] // len(v_heads)
    rows = pb.shape[0] // N_KV_HEADS
    parts = []
    for h in range(N_KV_HEADS):
        pv = None
        for i, vh in enumerate(v_heads):
            part = jnp.dot(pb[h * rows:(h + 1) * rows, i * nk:(i + 1) * nk], vh(h), preferred_element_type=F32)
            pv = part if pv is None else pv + part
        parts.append(pv)
    acc_ref[...] = alpha * acc_ref[...] + jnp.concatenate(parts, axis=0)
    m_ref[...] = m_new


def _decode_attn_kernel(pt_ref, q_ref, bias_ref, knew_ref, vnew_ref, bnew_ref, lamp_ref, subg_ref, *rest,
                        pages, page, lam_init):
    del pt_ref
    k_refs, v_refs = rest[:pages], rest[pages:2 * pages]
    o_ref, m_scr, l_scr, acc_scr = rest[2 * pages:]
    step = pl.program_id(1)

    @pl.when(step == 0)
    def _():
        m_scr[...] = jnp.full(m_scr.shape, M_INIT, F32)
        l_scr[...] = jnp.zeros(l_scr.shape, F32)
        acc_scr[...] = jnp.zeros(acc_scr.shape, F32)

    q = q_ref[...]
    rows = q.shape[0] // N_KV_HEADS
    nt = (((1,), (1,)), ((), ()))

    def head(ref, h):
        return ref[pl.ds(h, page, stride=N_KV_HEADS), :].astype(BF16)

    def logits(ref):
        return jnp.concatenate(
            [lax.dot_general(q[h * rows:(h + 1) * rows], head(ref, h), nt, preferred_element_type=F32)
             for h in range(N_KV_HEADS)], axis=0)

    s = jnp.concatenate([logits(k_refs[i]) for i in range(pages)], axis=1) + bias_ref[...]
    _decode_softmax_step(s, [functools.partial(head, v_refs[i]) for i in range(pages)], m_scr, l_scr, acc_scr)

    @pl.when(step == pl.num_programs(1) - 1)
    def _():
        s_new = logits(knew_ref) + bnew_ref[...]
        _decode_softmax_step(s_new, [functools.partial(head, vnew_ref)], m_scr, l_scr, acc_scr)
        lam = _lambda(lamp_ref, lam_init)
        out = acc_scr[...] / l_scr[...]
        for h in range(N_KV_HEADS):
            blk = out[h * rows:(h + 1) * rows]
            o = blk[:rows // 2] - lam * blk[rows // 2:]
            o = o * lax.rsqrt(jnp.mean(o * o, axis=-1, keepdims=True) + NORM_EPS) * subg_ref[...] * (1.0 - lam_init)
            o_ref[h * (rows // 2):(h + 1) * (rows // 2), :] = o.astype(o_ref.dtype)


DECODE_PAGES = 16


def _decode_bias(rel_bias, t_len, page):
    rows = N_KV_HEADS * 2 * GROUP * t_len
    pages = DECODE_PAGES
    row_h = (jnp.arange(N_KV_HEADS)[:, None, None, None] * GROUP + jnp.arange(GROUP)[None, None, :, None]
             + jnp.zeros((1, 2, 1, t_len), jnp.int32)).reshape(rows)
    row_t = (jnp.zeros((N_KV_HEADS, 2, GROUP, 1), jnp.int32) + jnp.arange(t_len, dtype=jnp.int32)).reshape(rows)
    key = jnp.arange(page, dtype=jnp.int32)[None, :]
    sel = jax.nn.one_hot(row_h, N_HEADS, dtype=F32)
    last = jnp.sum(jnp.where(sel.T[:, :, None] > 0, _bias_of_distance(rel_bias, page + row_t[:, None] - key), 0.0),
                   axis=0)
    bias_steps = jnp.zeros((2, rows, pages * page), F32).at[1, :, (pages - 1) * page:].set(last)
    n_new = jnp.where(key < t_len, row_t[:, None] - key, -1)
    b_all = _bias_of_distance(rel_bias, n_new)
    b_new = jnp.sum(jnp.where(sel.T[:, :, None] > 0, b_all, 0.0), axis=0)
    return bias_steps, b_new


def _decode_attention(q, k_new, v_new, cache_k, cache_v, layer, page_table, bias_steps, b_new, lamp, subg, lam_init):
    batch, n_pages = page_table.shape
    depth, n_pool, page, kvh, kdim = cache_k.shape
    t_len = q.shape[0] // batch
    rows = N_KV_HEADS * 2 * GROUP * t_len
    pages = DECODE_PAGES
    assert n_pages % pages == 0 and page >= MAX_DISTANCE and t_len <= page
    assert kvh == N_KV_HEADS and kdim == V_DIM
    n_steps = n_pages // pages
    prow = page * kvh
    ck = cache_k.reshape(depth, n_pool, prow, kdim)
    cv = cache_v.reshape(depth, n_pool, prow, kdim)

    q6 = q.reshape(batch, t_len, N_KV_HEADS, GROUP, 2, HEAD_DIM).transpose(0, 2, 4, 3, 1, 5)
    eye_m = jnp.eye(2, dtype=q.dtype)[None, None, :, None, None, :, None]
    q_all = (q6[:, :, :, :, :, None, :] * eye_m).reshape(batch, rows, 2 * HEAD_DIM)

    pad = ((0, 0), (0, page - t_len), (0, 0))
    k_pad = jnp.pad(k_new.reshape(batch, t_len, kvh * kdim), pad).reshape(batch, prow, kdim)
    v_pad = jnp.pad(v_new.reshape(batch, t_len, kvh * kdim), pad).reshape(batch, prow, kdim)

    def page_spec(i):
        return pl.BlockSpec((None, None, prow, kdim), lambda b, s, pt: (layer, pt[b, s * pages + i], 0, 0))

    grid_spec = pltpu.PrefetchScalarGridSpec(
        num_scalar_prefetch=1,
        grid=(batch, n_steps),
        in_specs=[
            pl.BlockSpec((None, rows, 2 * HEAD_DIM), lambda b, s, pt: (b, 0, 0)),
            pl.BlockSpec((None, rows, pages * page), lambda b, s, pt: (jnp.where(s == n_steps - 1, 1, 0), 0, 0)),
            pl.BlockSpec((None, prow, kdim), lambda b, s, pt: (b, 0, 0)),
            pl.BlockSpec((None, prow, kdim), lambda b, s, pt: (b, 0, 0)),
            pl.BlockSpec((rows, page), lambda b, s, pt: (0, 0)),
            pl.BlockSpec(lamp.shape, lambda b, s, pt: (0, 0)),
            pl.BlockSpec(subg.shape, lambda b, s, pt: (0, 0)),
        ] + [page_spec(i) for i in range(pages)] * 2,
        out_specs=pl.BlockSpec((None, rows // 2, V_DIM), lambda b, s, pt: (b, 0, 0)),
        scratch_shapes=[pltpu.VMEM((rows, 1), F32), pltpu.VMEM((rows, 1), F32), pltpu.VMEM((rows, V_DIM), F32)],
    )
    out = pl.pallas_call(
        functools.partial(_decode_attn_kernel, pages=pages, page=page, lam_init=lam_init),
        grid_spec=grid_spec,
        out_shape=jax.ShapeDtypeStruct((batch, rows // 2, V_DIM), BF16),
        compiler_params=_params(("arbitrary", "arbitrary"), 32),
    )(page_table, q_all, bias_steps, k_pad, v_pad, b_new, lamp, subg, *([ck] * pages), *([cv] * pages))
    out = out.reshape(batch, N_KV_HEADS, GROUP, t_len, V_DIM).transpose(0, 3, 1, 2, 4)
    return out.reshape(batch * t_len, N_HEADS * V_DIM)


def _gelu_tanh(x):
    return 0.5 * x * (1.0 + jnp.tanh(math.sqrt(2.0 / math.pi) * (x + 0.044715 * (x * x * x))))


def _lru_kernel(xr_ref, yr_ref, h0_ref, buf0_ref, cw_ref, cb_ref, wa_ref, ba_ref, wx_ref, bx_ref, lam_ref,
                o_ref, hlast_ref, tail_ref, xp_scr, h_scr, *, tb):
    pad = V7X_SUBLANES
    tail = CONV_WIDTH - 1

    @pl.when(pl.program_id(1) == 0)
    def _():
        h_scr[...] = h0_ref[...]
        xp_scr[pad - tail:pad, :] = buf0_ref[...]

    x = xr_ref[...]
    xp_scr[pad:pad + tb, :] = x
    cw = cw_ref[...]
    xc = cb_ref[...] + xp_scr[pad - tail:pad - tail + tb, :] * cw[0:1]
    for j in range(1, CONV_WIDTH):
        xc = xc + xp_scr[pad - tail + j:pad - tail + j + tb, :] * cw[j:j + 1]
    new_tail = xp_scr[pad + tb - tail:pad + tb, :]
    tail_ref[...] = new_tail
    xp_scr[pad - tail:pad, :] = new_tail

    xcb = xc.astype(BF16)
    nblk, bw = wa_ref.shape[0], wa_ref.shape[1]

    def gate(w_ref, b_ref):
        parts = [jnp.dot(xcb[:, c * bw:(c + 1) * bw], w_ref[c], preferred_element_type=F32) for c in range(nblk)]
        return _sigmoid(jnp.concatenate(parts, axis=1) + b_ref[...])

    r = gate(wa_ref, ba_ref)
    i = gate(wx_ref, bx_ref)
    neg_lam = -lam_ref[...]
    softplus = jnp.maximum(neg_lam, 0.0) + jnp.log(1.0 + jnp.exp(-jnp.abs(neg_lam)))
    log_a = -LRU_C * r * softplus
    a = jnp.exp(log_a)
    y = -jnp.tanh(log_a) * (a * a + 1.0)
    u = jnp.where(y > 0.0, y * lax.rsqrt(y), 0.0) * i * xc

    sub = V7X_SUBLANES
    nt = tb // sub
    big_a = a.reshape(nt, sub, a.shape[1])
    big_b = u.reshape(nt, sub, a.shape[1])
    row = lax.broadcasted_iota(jnp.int32, big_a.shape, 1)
    shift = 1
    while shift < sub:
        keep = row >= shift
        a_prev = jnp.where(keep, pltpu.roll(big_a, shift, 1), 1.0)
        b_prev = jnp.where(keep, pltpu.roll(big_b, shift, 1), 0.0)
        big_b = big_a * b_prev + big_b
        big_a = big_a * a_prev
        shift *= 2
    carry = h_scr[...]
    tiles = []
    for t in range(nt):
        h_t = big_a[t] * carry + big_b[t]
        carry = h_t[sub - 1:sub]
        tiles.append(h_t)
    h = jnp.concatenate(tiles, axis=0)
    h_scr[...] = h[tb - 1:tb]
    hlast_ref[...] = h[tb - 1:tb]
    o_ref[...] = (h * _gelu_tanh(yr_ref[...])).astype(o_ref.dtype)


def _lru(xr, yr, h0, buf0, conv_w, conv_b, wa, ba, wx, bx, lam, *, batch):
    m, w = xr.shape
    t_len = m // batch
    tb = min(256, t_len)
    nt = t_len // tb
    assert t_len % tb == 0 and tb % V7X_SUBLANES == 0
    tail = CONV_WIDTH - 1
    xr3, yr3 = xr.reshape(batch, t_len, w), yr.reshape(batch, t_len, w)
    rows = pl.BlockSpec((None, tb, w), lambda b, t: (b, t, 0))
    vec = pl.BlockSpec((1, w), lambda b, t: (0, 0))
    wspec = pl.BlockSpec(wa.shape, lambda b, t: (0, 0, 0))
    out, h_last, conv_tail = pl.pallas_call(
        functools.partial(_lru_kernel, tb=tb),
        grid=(batch, nt),
        in_specs=[rows, rows,
                  pl.BlockSpec((None, 1, w), lambda b, t: (b, 0, 0)),
                  pl.BlockSpec((None, tail, w), lambda b, t: (b, 0, 0)),
                  pl.BlockSpec((CONV_WIDTH, w), lambda b, t: (0, 0)), vec, wspec, vec, wspec, vec, vec],
        out_specs=[rows,
                   pl.BlockSpec((None, 1, w), lambda b, t: (b, 0, 0)),
                   pl.BlockSpec((None, tail, w), lambda b, t: (b, 0, 0))],
        out_shape=[jax.ShapeDtypeStruct((batch, t_len, w), BF16),
                   jax.ShapeDtypeStruct((batch, 1, w), F32),
                   jax.ShapeDtypeStruct((batch, tail, w), F32)],
        scratch_shapes=[pltpu.VMEM((V7X_SUBLANES + tb, w), F32), pltpu.VMEM((1, w), F32)],
        compiler_params=_params(("arbitrary", "arbitrary"), 48),
    )(xr3, yr3, h0.reshape(batch, 1, w), buf0, conv_w, conv_b, wa, ba, wx, bx, lam)
    return out.reshape(m, w), h_last.reshape(batch, w), conv_tail


def _block_diag(w, per):
    n, d, e = w.shape
    wg = w.reshape(n // per, per, d, e)
    eye = jnp.eye(per, dtype=w.dtype)
    return jnp.einsum('gpde,pq->gpdqe', wg, eye).reshape(n // per, per * d, per * e)


def _merge_kernel(x_ref, attn_ref, lru_ref, gla_ref, glr_ref, bga_ref, bgr_ref, g_ref, wpa_ref, wpr_ref, wo_ref,
                  o_ref):
    pa = jnp.dot(attn_ref[...], wpa_ref[...], preferred_element_type=F32)
    pr = jnp.dot(lru_ref[...], wpr_ref[...], preferred_element_type=F32)
    merged = _sigmoid(gla_ref[...] + bga_ref[...]) * pa + _sigmoid(glr_ref[...] + bgr_ref[...]) * pr
    o_ref[...] = x_ref[...] + g_ref[...] * jnp.dot(merged.astype(BF16), wo_ref[...], preferred_element_type=F32)


def _merge(x, attn, lru_out, gl, b_gate, g, w_pa, w_pr, w_o):
    m, d = x.shape
    tm = min(512, m)
    assert m % tm == 0
    row = pl.BlockSpec((tm, d), lambda i: (i, 0))
    wspec = _resident((d, d), lambda i: (0, 0))
    return pl.pallas_call(
        _merge_kernel,
        grid=(m // tm,),
        in_specs=[row, row, row, row, pl.BlockSpec((tm, d), lambda i: (i, 1)),
                  pl.BlockSpec((1, d), lambda i: (0, 0)), pl.BlockSpec((1, d), lambda i: (0, 1)),
                  _mod_spec(g, tm), wspec, wspec, wspec],
        out_specs=row,
        out_shape=jax.ShapeDtypeStruct((m, d), F32),
        compiler_params=_params(("arbitrary",), 48),
    )(x, attn, lru_out, gl, gl, b_gate, b_gate, g, w_pa, w_pr, w_o)


def _run_trunk(x, mods, p, *, batch, cache=None):
    depth = p['w_in'].shape[0]
    d = x.shape[1]
    ks, vs, hs, convs = [], [], [], []
    for l in range(depth):
        sh1, sc1, g1, sh2, sc2, g2, sh3, sc3, g3 = mods[l]
        lam_init = 0.8 - 0.6 * math.exp(-0.3 * l)
        lamp = p['lamp'][l]
        subg = p['attn_sub_g'][l][None, :]
        x = _ffn(x, sh1, sc1, g1, p['ln_ffn1'][l][None], p['ln_final'][None], p['ffn1_gu'][l], p['ffn1_down'][l],
                 final_norm=False)
        if cache is None:
            qt, k, v, kb, vt, xr, yr, gl = _mixin(x, sh2, sc2, p['ln_mix'][l][None], p['w_in'][l], prompt=True)
            attn = _prompt_attention(qt, kb, vt, p['prompt_bias'], lamp, subg, lam_init)
            h0 = jnp.zeros((batch, d), F32)
            buf0 = jnp.zeros((batch, CONV_WIDTH - 1, d), F32)
        else:
            cache_k, cache_v, page_table, state_h, state_conv = cache
            q, k, v, xr, yr, gl = _mixin(x, sh2, sc2, p['ln_mix'][l][None], p['w_in'][l], prompt=False)
            attn = _decode_attention(q, k, v, cache_k, cache_v, l, page_table, *p['decode_bias'], lamp, subg,
                                     lam_init)
            h0, buf0 = state_h[l], state_conv[l]
        lru_out, h_last, conv_tail = _lru(xr, yr, h0, buf0, p['conv_w'][l], p['conv_b'][l][None], p['rg_wa'][l],
                                          p['rg_ba'][l][None], p['rg_wx'][l], p['rg_bx'][l][None],
                                          p['rg_lambda'][l][None], batch=batch)
        x = _merge(x, attn, lru_out, gl, p['b_gate'][l][None], g2, p['w_pa'][l], p['w_pr'][l], p['w_o'][l])
        x = _ffn(x, sh3, sc3, g3, p['ln_ffn2'][l][None], p['ln_final'][None], p['ffn2_gu'][l], p['ffn2_down'][l],
                 final_norm=(l == depth - 1))
        t_len = x.shape[0] // batch
        ks.append(k.reshape(batch, t_len, N_KV_HEADS, 2 * HEAD_DIM))
        vs.append(v.reshape(batch, t_len, N_KV_HEADS, V_DIM))
        hs.append(h_last)
        convs.append(conv_tail)
    return x, jnp.stack(ks), jnp.stack(vs), jnp.stack(hs), jnp.stack(convs)


def kernel(x_prompt, x_sample, cache_k, cache_v, state_h, state_conv, page_table, c_prompt, c_sample, rel_bias, ada_w, ada_b, ln_ffn1, ffn1_gu, ffn1_down, ln_mix, w_in, b_gate, lambda_q1, lambda_k1, lambda_q2, lambda_k2, attn_sub_g, conv_w, conv_b, rg_wa, rg_ba, rg_wx, rg_bx, rg_lambda, w_pa, w_pr, w_o, ln_ffn2, ffn2_gu, ffn2_down, ln_final):
    b_p, s_len, d = x_prompt.shape
    b_d, t_dec, _ = x_sample.shape
    depth = w_in.shape[0]
    per = V7X_MXU_DIM // (rg_wa.shape[-1])
    p = dict(
        rel_bias=rel_bias, ln_ffn1=ln_ffn1, ln_mix=ln_mix, ln_ffn2=ln_ffn2, ln_final=ln_final, b_gate=b_gate,
        attn_sub_g=attn_sub_g, conv_w=conv_w, conv_b=conv_b, rg_ba=rg_ba, rg_bx=rg_bx, rg_lambda=rg_lambda,
        ffn1_gu=ffn1_gu.astype(BF16), ffn1_down=ffn1_down.astype(BF16), w_in=w_in.astype(BF16),
        w_pa=w_pa.astype(BF16), w_pr=w_pr.astype(BF16), w_o=w_o.astype(BF16),
        ffn2_gu=ffn2_gu.astype(BF16), ffn2_down=ffn2_down.astype(BF16),
        rg_wa=jnp.stack([_block_diag(rg_wa[l].astype(BF16), per) for l in range(depth)]),
        rg_wx=jnp.stack([_block_diag(rg_wx[l].astype(BF16), per) for l in range(depth)]),
        lamp=jnp.stack([lambda_q1, lambda_k1, lambda_q2, lambda_k2], axis=1).astype(F32),
        prompt_bias=_prompt_bias(rel_bias, min(ATTN_TILE, s_len), min(ATTN_TILE, s_len)),
        decode_bias=_decode_bias(rel_bias, t_dec, cache_k.shape[2]),
    )

    n_seq = b_p + b_d
    pad = (-n_seq) % V7X_SUBLANES
    c_all = jnp.pad(jnp.concatenate([c_prompt, c_sample], axis=0), ((0, pad), (0, 0)))
    mods = _ada_mods(c_all, ada_w, ada_b)

    def split(rows, repeat):
        out = []
        for l in range(depth):
            parts = [rows[l][:, k * d:(k + 1) * d] for k in range(N_MOD)]
            out.append([jnp.repeat(m, repeat, axis=0) if repeat > 1 else m for m in parts])
        return out

    assert b_p == 1
    mods_p = split(mods[:, :b_p], 1)
    mods_d = split(mods[:, b_p:n_seq], t_dec)

    y_p, k_p, v_p, h_p, conv_p = _run_trunk(x_prompt.reshape(b_p * s_len, d), mods_p, p, batch=b_p)
    y_d, k_d, v_d, h_d, conv_d = _run_trunk(x_sample.reshape(b_d * t_dec, d), mods_d, p, batch=b_d,
                                            cache=(cache_k, cache_v, page_table, state_h, state_conv))
    return (y_p.reshape(b_p, s_len, d), y_d.reshape(b_d, t_dec, d), k_p, v_p, h_p, conv_p, k_d, v_d, h_d, conv_d)
```

```python
import functools
import math

import jax
import jax.numpy as jnp
import numpy as np
from jax import lax
from jax.experimental import pallas as pl
from jax.experimental.pallas import tpu as pltpu

F32 = jnp.float32
BF16 = jnp.bfloat16

N_HEADS = 8
N_KV_HEADS = 4
GROUP = N_HEADS // N_KV_HEADS
HEAD_DIM = 64
V_DIM = 2 * HEAD_DIM
LRU_BLOCKS = 16
CONV_WIDTH = 4
LRU_C = 8.0
N_BUCKETS = 32
MAX_DISTANCE = 128
NORM_EPS = 1e-6
N_MOD = 9

V7X_LANES = 128
V7X_SUBLANES = 8
V7X_MXU_DIM = 256
V7X_VMEM_BYTES = 64 * 1024 * 1024
MIB = 1024 * 1024

ATTN_TILE = V7X_MXU_DIM


def _params(semantics, vmem_mib):
    assert vmem_mib * MIB < V7X_VMEM_BYTES
    return pltpu.CompilerParams(dimension_semantics=semantics, vmem_limit_bytes=vmem_mib * MIB)


def _resident(shape, index_map):
    return pl.BlockSpec(shape, index_map, pipeline_mode=pl.Buffered(1))


def _sigmoid(x):
    return 1.0 / (1.0 + jnp.exp(-x))


def _modulated_norm(x, ln, sc, sh):
    y = x * lax.rsqrt(jnp.mean(x * x, axis=-1, keepdims=True) + NORM_EPS) * ln
    return y * (1.0 + sc) + sh


def _mod_spec(mod, tm):
    if mod.shape[0] == 1:
        return pl.BlockSpec((1, mod.shape[1]), lambda i: (0, 0))
    return pl.BlockSpec((tm, mod.shape[1]), lambda i: (i, 0))


def _ada_kernel(c_ref, w_ref, b_ref, o_ref):
    c = c_ref[...]
    act = c * _sigmoid(c)
    o_ref[...] = jnp.dot(act, w_ref[...], preferred_element_type=F32) + b_ref[...]


def _ada_mods(c_all, ada_w, ada_b):
    depth, d, n = ada_w.shape
    mp = c_all.shape[0]
    tn = n // 8
    assert tn % V7X_LANES == 0 and mp % V7X_SUBLANES == 0
    return pl.pallas_call(
        _ada_kernel,
        grid=(depth, n // tn),
        in_specs=[
            pl.BlockSpec((mp, d), lambda l, j: (0, 0)),
            pl.BlockSpec((None, d, tn), lambda l, j: (l, 0, j)),
            pl.BlockSpec((None, 1, tn), lambda l, j: (l, 0, j)),
        ],
        out_specs=pl.BlockSpec((None, mp, tn), lambda l, j: (l, 0, j)),
        out_shape=jax.ShapeDtypeStruct((depth, mp, n), F32),
        compiler_params=_params(("arbitrary", "arbitrary"), 24),
    )(c_all, ada_w, ada_b.reshape(depth, 1, n))


def _ffn_kernel(x_ref, sh_ref, sc_ref, g_ref, ln_ref, lnf_ref, wgu_ref, wd_ref, o_ref, *, d_ff, tf, final_norm):
    x = x_ref[...]
    h = _modulated_norm(x, ln_ref[...], sc_ref[...], sh_ref[...]).astype(BF16)
    acc = jnp.zeros(x.shape, F32)
    for c in range(d_ff // tf):
        gate = jnp.dot(h, wgu_ref[:, c * tf:(c + 1) * tf], preferred_element_type=F32)
        up = jnp.dot(h, wgu_ref[:, d_ff + c * tf:d_ff + (c + 1) * tf], preferred_element_type=F32)
        act = (gate * _sigmoid(gate) * up).astype(BF16)
        acc = acc + jnp.dot(act, wd_ref[c * tf:(c + 1) * tf, :], preferred_element_type=F32)
    y = x + 0.5 * g_ref[...] * acc
    if final_norm:
        y = y * lax.rsqrt(jnp.mean(y * y, axis=-1, keepdims=True) + NORM_EPS) * lnf_ref[...]
    o_ref[...] = y


def _ffn(x, sh, sc, g, ln, lnf, w_gu, w_down, *, final_norm):
    m, d = x.shape
    d_ff = w_down.shape[0]
    tm = min(512, m)
    tf = V7X_MXU_DIM
    assert m % tm == 0 and d_ff % tf == 0
    row = pl.BlockSpec((tm, d), lambda i: (i, 0))
    vec = pl.BlockSpec((1, d), lambda i: (0, 0))
    return pl.pallas_call(
        functools.partial(_ffn_kernel, d_ff=d_ff, tf=tf, final_norm=final_norm),
        grid=(m // tm,),
        in_specs=[row, _mod_spec(sh, tm), _mod_spec(sc, tm), _mod_spec(g, tm), vec, vec,
                  _resident(w_gu.shape, lambda i: (0, 0)), _resident(w_down.shape, lambda i: (0, 0))],
        out_specs=row,
        out_shape=jax.ShapeDtypeStruct((m, d), F32),
        compiler_params=_params(("arbitrary",), 48),
    )(x, sh, sc, g, ln, lnf, w_gu, w_down)


def _mixin_kernel(x_ref, sh_ref, sc_ref, ln_ref, w_ref, *rest, prompt, n_alias):
    out_refs = rest[n_alias:]
    h = _modulated_norm(x_ref[...], ln_ref[...], sc_ref[...], sh_ref[...]).astype(BF16)
    if prompt:
        qt_ref, k_ref, v_ref, kb_ref, vt_ref, xr_ref, yr_ref, gl_ref = out_refs
    else:
        q_ref, k_ref, v_ref, xr_ref, yr_ref, gl_ref = out_refs
    q_cols, kv_cols = N_HEADS * 2 * HEAD_DIM, N_KV_HEADS * V_DIM
    tm = x_ref.shape[0]

    def proj(col, width):
        return jnp.dot(h, w_ref[:, col:col + width], preferred_element_type=F32)

    q = proj(0, q_cols) * HEAD_DIM ** -0.5
    k = proj(q_cols, kv_cols)
    v = proj(q_cols + kv_cols, kv_cols)
    for hh in range(N_KV_HEADS):
        k_ref[pl.ds(hh, tm, stride=N_KV_HEADS), :] = k[:, hh * V_DIM:(hh + 1) * V_DIM]
        v_ref[pl.ds(hh, tm, stride=N_KV_HEADS), :] = v[:, hh * V_DIM:(hh + 1) * V_DIM]
    if prompt:
        qt_ref[...] = q.T.astype(BF16)
        kb_ref[...] = k.astype(BF16)
        pad_row = lax.broadcasted_iota(jnp.int32, (ACC_ROWS - V_DIM, v.shape[0]), 0)
        ones = jnp.where(pad_row == 0, 1.0, 0.0).astype(BF16)
        for hh in range(N_KV_HEADS):
            vt_ref[hh, :V_DIM, :] = v[:, hh * V_DIM:(hh + 1) * V_DIM].T.astype(BF16)
            vt_ref[hh, V_DIM:, :] = ones
    else:
        q_ref[...] = q.astype(BF16)
    col = q_cols + 2 * kv_cols
    for ref in (xr_ref, yr_ref, gl_ref):
        ref[...] = proj(col, ref.shape[1])
        col += ref.shape[1]


def _mixin(x, sh, sc, ln, w_in, *, prompt, layer, depth, kv_prev):
    m, d = x.shape
    q_cols, kv_cols = N_HEADS * 2 * HEAD_DIM, N_KV_HEADS * V_DIM
    tm = min(ATTN_TILE, m)
    assert m % tm == 0 and w_in.shape[1] == q_cols + 2 * kv_cols + 4 * d

    def rows(width, dtype):
        return pl.BlockSpec((tm, width), lambda i: (i, 0)), jax.ShapeDtypeStruct((m, width), dtype)

    kv_out = (pl.BlockSpec((None, tm * N_KV_HEADS, V_DIM), lambda i: (layer, i, 0)),
              jax.ShapeDtypeStruct((depth, m * N_KV_HEADS, V_DIM), F32))
    if prompt:
        outs = [(pl.BlockSpec((q_cols, tm), lambda i: (0, i)), jax.ShapeDtypeStruct((q_cols, m), BF16)),
                kv_out, kv_out, rows(kv_cols, BF16),
                (pl.BlockSpec((N_KV_HEADS, None, ACC_ROWS, tm), lambda i: (0, i, 0, 0)),
                 jax.ShapeDtypeStruct((N_KV_HEADS, m // tm, ACC_ROWS, tm), BF16))]
    else:
        outs = [rows(q_cols, BF16), kv_out, kv_out]
    outs += [rows(d, F32), rows(d, F32), rows(2 * d, F32)]
    in_specs = [pl.BlockSpec((tm, d), lambda i: (i, 0)), _mod_spec(sh, tm), _mod_spec(sc, tm),
                pl.BlockSpec((1, d), lambda i: (0, 0)), _resident(w_in.shape, lambda i: (0, 0))]
    aliases = {} if kv_prev is None else {len(in_specs): 1, len(in_specs) + 1: 2}
    extra = () if kv_prev is None else tuple(kv_prev)
    in_specs += [pl.BlockSpec(memory_space=pl.ANY)] * len(extra)
    return pl.pallas_call(
        functools.partial(_mixin_kernel, prompt=prompt, n_alias=len(extra)),
        grid=(m // tm,),
        in_specs=in_specs,
        out_specs=[o[0] for o in outs],
        out_shape=[o[1] for o in outs],
        input_output_aliases=aliases,
        compiler_params=_params(("arbitrary",), 48),
    )(x, sh, sc, ln, w_in, *extra)


def _rel_bucket(n):
    max_exact = N_BUCKETS // 2
    nf = jnp.maximum(n, 1).astype(F32)
    large = max_exact + (jnp.log(nf / max_exact) / math.log(MAX_DISTANCE / max_exact)
                         * (N_BUCKETS - max_exact)).astype(jnp.int32)
    large = jnp.minimum(large, N_BUCKETS - 1)
    return jnp.where(n < max_exact, n, large)


def _bias_of_distance(rel_bias, n):
    table = rel_bias.astype(F32) - rel_bias[N_BUCKETS - 1].astype(F32)
    onehot = (_rel_bucket(jnp.maximum(n, 0))[..., None] == jnp.arange(N_BUCKETS, dtype=jnp.int32)).astype(F32)
    b = jnp.einsum('...b,bh->h...', onehot, table, precision=lax.Precision.HIGHEST)
    return jnp.where(n >= 0, b, -jnp.inf)


def _lambda(lamp_ref, lam_init):
    lp = lamp_ref[...]
    return jnp.exp(jnp.sum(lp[0:1] * lp[1:2])) - jnp.exp(jnp.sum(lp[2:3] * lp[3:4])) + lam_init


M_INIT = -1e30


ACC_ROWS = V_DIM + 16
LONG_RUN = 8
SHORT_RUN = 4


def _col_max(s):
    rows = s.shape[0]
    while rows > V7X_SUBLANES:
        rows //= 2
        s = jnp.maximum(s[:rows], s[rows:])
    return jnp.max(s, axis=0, keepdims=True)


def _prompt_attn_kernel(qt_ref, k_ref, vt_ref, bias_ref, lamp_ref, subg_ref, o_ref,
                        qz_scr, sa_scr, sb_scr, m_scr, acc_scr, *, tq, tk, lam_init):
    qi = pl.program_id(1)
    ncomb = 2 * GROUP
    ratio = tq // tk
    row = lax.broadcasted_iota(jnp.int32, (V_DIM, tq), 0)
    for g in range(GROUP):
        qg = qt_ref[g * V_DIM:(g + 1) * V_DIM, :]
        qz_scr[:, (2 * g) * tq:(2 * g + 1) * tq] = jnp.where(row < HEAD_DIM, qg, jnp.zeros_like(qg))
        qz_scr[:, (2 * g + 1) * tq:(2 * g + 2) * tq] = jnp.where(row >= HEAD_DIM, qg, jnp.zeros_like(qg))
    m_scr[...] = jnp.full(m_scr.shape, M_INIT, F32)
    acc_scr[...] = jnp.zeros(acc_scr.shape, F32)
    s_bufs = (sa_scr, sb_scr)

    def logits(j, dst):
        kb = k_ref[pl.ds(pl.multiple_of(j * tk, tk), tk), :]
        dst[...] = jnp.dot(kb, qz_scr[...], preferred_element_type=F32)

    def update(j, src, bias_row):
        vt = vt_ref[j]
        for c in range(ncomb):
            s = src[:, c * tq:(c + 1) * tq]
            if bias_row is not None:
                g = c // 2
                s = s + bias_ref[bias_row:bias_row + tk, g * tq:(g + 1) * tq]
            m_old = m_scr[c]
            m_new = jnp.maximum(m_old, _col_max(s))
            alpha = jnp.exp(m_old - m_new)
            p = jnp.exp(s - m_new).astype(BF16)
            acc_scr[c] = alpha * acc_scr[c] + jnp.dot(vt, p, preferred_element_type=F32)
            m_scr[c] = m_new

    def run(j, count):
        for u in range(count):
            logits(j + u + 1, s_bufs[(u + 1) % 2])
            update(j + u, s_bufs[u % 2], None)

    def by_parity(j, fn):
        for parity in range(2):
            @pl.when(j % 2 == parity)
            def _():
                fn(s_bufs[parity], s_bufs[1 - parity])

    logits(0, sa_scr)

    first = ratio * qi - 1
    n_far = jnp.maximum(first, 0)
    n_long = n_far // LONG_RUN

    def long_runs(t, carry):
        run(t * LONG_RUN, LONG_RUN)
        return carry

    lax.fori_loop(0, n_long, long_runs, 0)
    done = n_long * LONG_RUN
    short = n_far - done >= SHORT_RUN

    @pl.when(short)
    def _():
        run(done, SHORT_RUN)

    done = done + jnp.where(short, SHORT_RUN, 0)

    def single(j, carry):
        def body(cur, nxt):
            logits(j + 1, nxt)
            update(j, cur, None)
        by_parity(j, body)
        return carry

    lax.fori_loop(done, n_far, single, 0)

    def tail_with_near(cur, nxt):
        bufs = (cur, nxt)
        for t in range(ratio + 1):
            if t < ratio:
                logits(first + t + 1, bufs[(t + 1) % 2])
            update(first + t, bufs[t % 2], t * tk)

    def tail_without_near(cur, nxt):
        bufs = (cur, nxt)
        for t in range(ratio):
            if t + 1 < ratio:
                logits(t + 1, bufs[(t + 1) % 2])
            update(t, bufs[t % 2], (t + 1) * tk)

    @pl.when(qi > 0)
    def _():
        by_parity(first, tail_with_near)

    @pl.when(qi == 0)
    def _():
        tail_without_near(sa_scr, sb_scr)

    lam = _lambda(lamp_ref, lam_init)
    for g in range(GROUP):
        a0, a1 = acc_scr[2 * g], acc_scr[2 * g + 1]
        o = a0[:V_DIM] * (1.0 / a0[V_DIM:V_DIM + 1]) - lam * (a1[:V_DIM] * (1.0 / a1[V_DIM:V_DIM + 1]))
        o = o * lax.rsqrt(jnp.mean(o * o, axis=0, keepdims=True) + NORM_EPS) * subg_ref[...] * (1.0 - lam_init)
        o_ref[:, g * V_DIM:(g + 1) * V_DIM] = o.T.astype(o_ref.dtype)


def _prompt_bias(rel_bias, tq, tk):
    nb = (tq // tk + 1) * tk
    c = jnp.arange(nb, dtype=jnp.int32)[:, None]
    r = jnp.arange(tq, dtype=jnp.int32)[None, :]
    bias = _bias_of_distance(rel_bias, tk + r - c)
    return bias.reshape(N_KV_HEADS, GROUP, nb, tq).transpose(0, 2, 1, 3).reshape(N_KV_HEADS, nb, GROUP * tq)


def _prompt_attention(qt, kb, vt, bias, lamp, subg, lam_init):
    s_len = qt.shape[1]
    nk, tk = vt.shape[1], vt.shape[3]
    tq = tk
    nq = s_len // tq
    ratio = tq // tk
    nb = bias.shape[1]
    assert s_len == nq * tq == nk * tk and tk >= MAX_DISTANCE and nb == (ratio + 1) * tk
    subg_col = subg.reshape(V_DIM, 1)
    ncomb = 2 * GROUP
    return pl.pallas_call(
        functools.partial(_prompt_attn_kernel, tq=tq, tk=tk, lam_init=lam_init),
        grid=(N_KV_HEADS, nq),
        in_specs=[
            pl.BlockSpec((GROUP * V_DIM, tq), lambda h, i: (h, i)),
            pl.BlockSpec((s_len, V_DIM), lambda h, i: (0, h)),
            pl.BlockSpec((None, nk, ACC_ROWS, tk), lambda h, i: (h, 0, 0, 0)),
            pl.BlockSpec((None, nb, GROUP * tq), lambda h, i: (h, 0, 0)),
            pl.BlockSpec(lamp.shape, lambda h, i: (0, 0)),
            pl.BlockSpec(subg_col.shape, lambda h, i: (0, 0)),
        ],
        out_specs=pl.BlockSpec((tq, GROUP * V_DIM), lambda h, i: (i, h)),
        out_shape=jax.ShapeDtypeStruct((s_len, N_HEADS * V_DIM), BF16),
        scratch_shapes=[pltpu.VMEM((V_DIM, ncomb * tq), BF16),
                        pltpu.VMEM((tk, ncomb * tq), F32), pltpu.VMEM((tk, ncomb * tq), F32),
                        pltpu.VMEM((ncomb, 1, tq), F32), pltpu.VMEM((ncomb, ACC_ROWS, tq), F32)],
        compiler_params=_params(("arbitrary", "arbitrary"), 40),
    )(qt, kb, vt, bias, lamp, subg_col)


def _decode_softmax_step(s, v_heads, m_ref, l_ref, acc_ref):
    m_old = m_ref[...]
    m_new = jnp.maximum(m_old, jnp.max(s, axis=-1, keepdims=True))
    alpha = jnp.exp(m_old - m_new)
    p = jnp.exp(s - m_new)
    l_ref[...] = alpha * l_ref[...] + jnp.sum(p, axis=-1, keepdims=True)
    pb = p.astype(BF16)
    nk = pb.shape[1] // len(v_heads)
    rows = pb.shape[0] // N_KV_HEADS
    parts = []
    for h in range(N_KV_HEADS):
        pv = None
        for i, vh in enumerate(v_heads):
            part = jnp.dot(pb[h * rows:(h + 1) * rows, i * nk:(i + 1) * nk], vh(h), preferred_element_type=F32)
            pv = part if pv is None else pv + part
        parts.append(pv)
    acc_ref[...] = alpha * acc_ref[...] + jnp.concatenate(parts, axis=0)
    m_ref[...] = m_new


def _decode_attn_kernel(pt_ref, q_ref, bias_ref, knew_ref, vnew_ref, bnew_ref, lamp_ref, subg_ref, *rest,
                        pages, page, lam_init):
    del pt_ref
    k_refs, v_refs = rest[:pages], rest[pages:2 * pages]
    o_ref, m_scr, l_scr, acc_scr = rest[2 * pages:]
    step = pl.program_id(1)

    @pl.when(step == 0)
    def _():
        m_scr[...] = jnp.full(m_scr.shape, M_INIT, F32)
        l_scr[...] = jnp.zeros(l_scr.shape, F32)
        acc_scr[...] = jnp.zeros(acc_scr.shape, F32)

    q = q_ref[...]
    rows = q.shape[0] // N_KV_HEADS
    nt = (((1,), (1,)), ((), ()))

    def head(ref, h):
        return ref[pl.ds(h, page, stride=N_KV_HEADS), :].astype(BF16)

    def logits(ref):
        return jnp.concatenate(
            [lax.dot_general(q[h * rows:(h + 1) * rows], head(ref, h), nt, preferred_element_type=F32)
             for h in range(N_KV_HEADS)], axis=0)

    s = jnp.concatenate([logits(k_refs[i]) for i in range(pages)], axis=1) + bias_ref[...]
    _decode_softmax_step(s, [functools.partial(head, v_refs[i]) for i in range(pages)], m_scr, l_scr, acc_scr)

    @pl.when(step == pl.num_programs(1) - 1)
    def _():
        s_new = logits(knew_ref) + bnew_ref[...]
        _decode_softmax_step(s_new, [functools.partial(head, vnew_ref)], m_scr, l_scr, acc_scr)
        lam = _lambda(lamp_ref, lam_init)
        out = acc_scr[...] / l_scr[...]
        for h in range(N_KV_HEADS):
            blk = out[h * rows:(h + 1) * rows]
            o = blk[:rows // 2] - lam * blk[rows // 2:]
            o = o * lax.rsqrt(jnp.mean(o * o, axis=-1, keepdims=True) + NORM_EPS) * subg_ref[...] * (1.0 - lam_init)
            o_ref[h * (rows // 2):(h + 1) * (rows // 2), :] = o.astype(o_ref.dtype)


DECODE_PAGES = 16


def _decode_bias(rel_bias, t_len, page):
    rows = N_KV_HEADS * 2 * GROUP * t_len
    pages = DECODE_PAGES
    row_h = (jnp.arange(N_KV_HEADS)[:, None, None, None] * GROUP + jnp.arange(GROUP)[None, None, :, None]
             + jnp.zeros((1, 2, 1, t_len), jnp.int32)).reshape(rows)
    row_t = (jnp.zeros((N_KV_HEADS, 2, GROUP, 1), jnp.int32) + jnp.arange(t_len, dtype=jnp.int32)).reshape(rows)
    key = jnp.arange(page, dtype=jnp.int32)[None, :]
    sel = jax.nn.one_hot(row_h, N_HEADS, dtype=F32)
    last = jnp.sum(jnp.where(sel.T[:, :, None] > 0, _bias_of_distance(rel_bias, page + row_t[:, None] - key), 0.0),
                   axis=0)
    bias_steps = jnp.zeros((2, rows, pages * page), F32).at[1, :, (pages - 1) * page:].set(last)
    n_new = jnp.where(key < t_len, row_t[:, None] - key, -1)
    b_all = _bias_of_distance(rel_bias, n_new)
    b_new = jnp.sum(jnp.where(sel.T[:, :, None] > 0, b_all, 0.0), axis=0)
    return bias_steps, b_new


def _decode_attention(q, k_new, v_new, cache_k, cache_v, layer, page_table, bias_steps, b_new, lamp, subg, lam_init):
    batch, n_pages = page_table.shape
    depth, n_pool, page, kvh, kdim = cache_k.shape
    t_len = q.shape[0] // batch
    rows = N_KV_HEADS * 2 * GROUP * t_len
    pages = DECODE_PAGES
    assert n_pages % pages == 0 and page >= MAX_DISTANCE and t_len <= page
    assert kvh == N_KV_HEADS and kdim == V_DIM
    n_steps = n_pages // pages
    prow = page * kvh
    ck = cache_k.reshape(depth, n_pool, prow, kdim)
    cv = cache_v.reshape(depth, n_pool, prow, kdim)

    q6 = q.reshape(batch, t_len, N_KV_HEADS, GROUP, 2, HEAD_DIM).transpose(0, 2, 4, 3, 1, 5)
    eye_m = jnp.eye(2, dtype=q.dtype)[None, None, :, None, None, :, None]
    q_all = (q6[:, :, :, :, :, None, :] * eye_m).reshape(batch, rows, 2 * HEAD_DIM)

    pad = ((0, 0), (0, (page - t_len) * kvh), (0, 0))
    k_pad = jnp.pad(k_new.reshape(batch, t_len * kvh, kdim), pad)
    v_pad = jnp.pad(v_new.reshape(batch, t_len * kvh, kdim), pad)

    def page_spec(i):
        return pl.BlockSpec((None, None, prow, kdim), lambda b, s, pt: (layer, pt[b, s * pages + i], 0, 0))

    grid_spec = pltpu.PrefetchScalarGridSpec(
        num_scalar_prefetch=1,
        grid=(batch, n_steps),
        in_specs=[
            pl.BlockSpec((None, rows, 2 * HEAD_DIM), lambda b, s, pt: (b, 0, 0)),
            pl.BlockSpec((None, rows, pages * page), lambda b, s, pt: (jnp.where(s == n_steps - 1, 1, 0), 0, 0)),
            pl.BlockSpec((None, prow, kdim), lambda b, s, pt: (b, 0, 0)),
            pl.BlockSpec((None, prow, kdim), lambda b, s, pt: (b, 0, 0)),
            pl.BlockSpec((rows, page), lambda b, s, pt: (0, 0)),
            pl.BlockSpec(lamp.shape, lambda b, s, pt: (0, 0)),
            pl.BlockSpec(subg.shape, lambda b, s, pt: (0, 0)),
        ] + [page_spec(i) for i in range(pages)] * 2,
        out_specs=pl.BlockSpec((None, rows // 2, V_DIM), lambda b, s, pt: (b, 0, 0)),
        scratch_shapes=[pltpu.VMEM((rows, 1), F32), pltpu.VMEM((rows, 1), F32), pltpu.VMEM((rows, V_DIM), F32)],
    )
    out = pl.pallas_call(
        functools.partial(_decode_attn_kernel, pages=pages, page=page, lam_init=lam_init),
        grid_spec=grid_spec,
        out_shape=jax.ShapeDtypeStruct((batch, rows // 2, V_DIM), BF16),
        compiler_params=_params(("arbitrary", "arbitrary"), 32),
    )(page_table, q_all, bias_steps, k_pad, v_pad, b_new, lamp, subg, *([ck] * pages), *([cv] * pages))
    out = out.reshape(batch, N_KV_HEADS, GROUP, t_len, V_DIM).transpose(0, 3, 1, 2, 4)
    return out.reshape(batch * t_len, N_HEADS * V_DIM)


def _gelu_tanh(x):
    return 0.5 * x * (1.0 + jnp.tanh(math.sqrt(2.0 / math.pi) * (x + 0.044715 * (x * x * x))))


def _lru_kernel(xr_ref, yr_ref, h0_ref, buf0_ref, cw_ref, cb_ref, wa_ref, ba_ref, wx_ref, bx_ref, lam_ref,
                o_ref, hlast_ref, tail_ref, xp_scr, h_scr, *, tb):
    pad = V7X_SUBLANES
    tail = CONV_WIDTH - 1

    @pl.when(pl.program_id(1) == 0)
    def _():
        h_scr[...] = h0_ref[...]
        xp_scr[pad - tail:pad, :] = buf0_ref[...]

    x = xr_ref[...]
    xp_scr[pad:pad + tb, :] = x
    cw = cw_ref[...]
    xc = cb_ref[...] + xp_scr[pad - tail:pad - tail + tb, :] * cw[0:1]
    for j in range(1, CONV_WIDTH):
        xc = xc + xp_scr[pad - tail + j:pad - tail + j + tb, :] * cw[j:j + 1]
    new_tail = xp_scr[pad + tb - tail:pad + tb, :]
    tail_ref[...] = new_tail
    xp_scr[pad - tail:pad, :] = new_tail

    xcb = xc.astype(BF16)
    nblk, bw = wa_ref.shape[0], wa_ref.shape[1]

    def gate(w_ref, b_ref):
        parts = [jnp.dot(xcb[:, c * bw:(c + 1) * bw], w_ref[c], preferred_element_type=F32) for c in range(nblk)]
        return _sigmoid(jnp.concatenate(parts, axis=1) + b_ref[...])

    r = gate(wa_ref, ba_ref)
    i = gate(wx_ref, bx_ref)
    neg_lam = -lam_ref[...]
    softplus = jnp.maximum(neg_lam, 0.0) + jnp.log(1.0 + jnp.exp(-jnp.abs(neg_lam)))
    log_a = -LRU_C * r * softplus
    a = jnp.exp(log_a)
    y = -jnp.tanh(log_a) * (a * a + 1.0)
    u = jnp.where(y > 0.0, y * lax.rsqrt(y), 0.0) * i * xc

    sub = V7X_SUBLANES
    nt = tb // sub
    big_a = a.reshape(nt, sub, a.shape[1])
    big_b = u.reshape(nt, sub, a.shape[1])
    row = lax.broadcasted_iota(jnp.int32, big_a.shape, 1)
    shift = 1
    while shift < sub:
        keep = row >= shift
        a_prev = jnp.where(keep, pltpu.roll(big_a, shift, 1), 1.0)
        b_prev = jnp.where(keep, pltpu.roll(big_b, shift, 1), 0.0)
        big_b = big_a * b_prev + big_b
        big_a = big_a * a_prev
        shift *= 2
    carry = h_scr[...]
    tiles = []
    for t in range(nt):
        h_t = big_a[t] * carry + big_b[t]
        carry = h_t[sub - 1:sub]
        tiles.append(h_t)
    h = jnp.concatenate(tiles, axis=0)
    h_scr[...] = h[tb - 1:tb]
    hlast_ref[...] = h[tb - 1:tb]
    o_ref[...] = (h * _gelu_tanh(yr_ref[...])).astype(o_ref.dtype)


def _lru(xr, yr, h0, buf0, conv_w, conv_b, wa, ba, wx, bx, lam, *, batch):
    m, w = xr.shape
    t_len = m // batch
    tb = min(256, t_len)
    nt = t_len // tb
    assert t_len % tb == 0 and tb % V7X_SUBLANES == 0
    tail = CONV_WIDTH - 1
    xr3, yr3 = xr.reshape(batch, t_len, w), yr.reshape(batch, t_len, w)
    rows = pl.BlockSpec((None, tb, w), lambda b, t: (b, t, 0))
    vec = pl.BlockSpec((1, w), lambda b, t: (0, 0))
    wspec = pl.BlockSpec(wa.shape, lambda b, t: (0, 0, 0))
    out, h_last, conv_tail = pl.pallas_call(
        functools.partial(_lru_kernel, tb=tb),
        grid=(batch, nt),
        in_specs=[rows, rows,
                  pl.BlockSpec((None, 1, w), lambda b, t: (b, 0, 0)),
                  pl.BlockSpec((None, tail, w), lambda b, t: (b, 0, 0)),
                  pl.BlockSpec((CONV_WIDTH, w), lambda b, t: (0, 0)), vec, wspec, vec, wspec, vec, vec],
        out_specs=[rows,
                   pl.BlockSpec((None, 1, w), lambda b, t: (b, 0, 0)),
                   pl.BlockSpec((None, tail, w), lambda b, t: (b, 0, 0))],
        out_shape=[jax.ShapeDtypeStruct((batch, t_len, w), BF16),
                   jax.ShapeDtypeStruct((batch, 1, w), F32),
                   jax.ShapeDtypeStruct((batch, tail, w), F32)],
        scratch_shapes=[pltpu.VMEM((V7X_SUBLANES + tb, w), F32), pltpu.VMEM((1, w), F32)],
        compiler_params=_params(("arbitrary", "arbitrary"), 48),
    )(xr3, yr3, h0.reshape(batch, 1, w), buf0, conv_w, conv_b, wa, ba, wx, bx, lam)
    return out.reshape(m, w), h_last.reshape(batch, w), conv_tail


def _block_diag(w, per):
    n, d, e = w.shape
    wg = w.reshape(n // per, per, d, e)
    eye = jnp.eye(per, dtype=w.dtype)
    return jnp.einsum('gpde,pq->gpdqe', wg, eye).reshape(n // per, per * d, per * e)


def _merge_kernel(x_ref, attn_ref, lru_ref, gla_ref, glr_ref, bga_ref, bgr_ref, g_ref, wpa_ref, wpr_ref, wo_ref,
                  o_ref):
    pa = jnp.dot(attn_ref[...], wpa_ref[...], preferred_element_type=F32)
    pr = jnp.dot(lru_ref[...], wpr_ref[...], preferred_element_type=F32)
    merged = _sigmoid(gla_ref[...] + bga_ref[...]) * pa + _sigmoid(glr_ref[...] + bgr_ref[...]) * pr
    o_ref[...] = x_ref[...] + g_ref[...] * jnp.dot(merged.astype(BF16), wo_ref[...], preferred_element_type=F32)


def _merge(x, attn, lru_out, gl, b_gate, g, w_pa, w_pr, w_o):
    m, d = x.shape
    tm = min(512, m)
    assert m % tm == 0
    row = pl.BlockSpec((tm, d), lambda i: (i, 0))
    wspec = _resident((d, d), lambda i: (0, 0))
    return pl.pallas_call(
        _merge_kernel,
        grid=(m // tm,),
        in_specs=[row, row, row, row, pl.BlockSpec((tm, d), lambda i: (i, 1)),
                  pl.BlockSpec((1, d), lambda i: (0, 0)), pl.BlockSpec((1, d), lambda i: (0, 1)),
                  _mod_spec(g, tm), wspec, wspec, wspec],
        out_specs=row,
        out_shape=jax.ShapeDtypeStruct((m, d), F32),
        compiler_params=_params(("arbitrary",), 48),
    )(x, attn, lru_out, gl, gl, b_gate, b_gate, g, w_pa, w_pr, w_o)


def _run_trunk(x, mods, p, *, batch, cache=None):
    depth = p['w_in'].shape[0]
    d = x.shape[1]
    hs, convs = [], []
    kv_all = None
    mix = functools.partial(_mixin, prompt=cache is None, depth=depth)
    for l in range(depth):
        sh1, sc1, g1, sh2, sc2, g2, sh3, sc3, g3 = mods[l]
        lam_init = 0.8 - 0.6 * math.exp(-0.3 * l)
        lamp = p['lamp'][l]
        subg = p['attn_sub_g'][l][None, :]
        x = _ffn(x, sh1, sc1, g1, p['ln_ffn1'][l][None], p['ln_final'][None], p['ffn1_gu'][l], p['ffn1_down'][l],
                 final_norm=False)
        if cache is None:
            qt, k_all, v_all, kb, vt, xr, yr, gl = mix(x, sh2, sc2, p['ln_mix'][l][None], p['w_in'][l], layer=l,
                                                       kv_prev=kv_all)
            attn = _prompt_attention(qt, kb, vt, p['prompt_bias'], lamp, subg, lam_init)
            h0 = jnp.zeros((batch, d), F32)
            buf0 = jnp.zeros((batch, CONV_WIDTH - 1, d), F32)
        else:
            cache_k, cache_v, page_table, state_h, state_conv = cache
            q, k_all, v_all, xr, yr, gl = mix(x, sh2, sc2, p['ln_mix'][l][None], p['w_in'][l], layer=l,
                                              kv_prev=kv_all)
            attn = _decode_attention(q, k_all[l], v_all[l], cache_k, cache_v, l, page_table, *p['decode_bias'],
                                     lamp, subg, lam_init)
            h0, buf0 = state_h[l], state_conv[l]
        kv_all = (k_all, v_all)
        lru_out, h_last, conv_tail = _lru(xr, yr, h0, buf0, p['conv_w'][l], p['conv_b'][l][None], p['rg_wa'][l],
                                          p['rg_ba'][l][None], p['rg_wx'][l], p['rg_bx'][l][None],
                                          p['rg_lambda'][l][None], batch=batch)
        x = _merge(x, attn, lru_out, gl, p['b_gate'][l][None], g2, p['w_pa'][l], p['w_pr'][l], p['w_o'][l])
        x = _ffn(x, sh3, sc3, g3, p['ln_ffn2'][l][None], p['ln_final'][None], p['ffn2_gu'][l], p['ffn2_down'][l],
                 final_norm=(l == depth - 1))
        hs.append(h_last)
        convs.append(conv_tail)
    kv_shape = (depth, batch, x.shape[0] // batch, N_KV_HEADS, V_DIM)
    return x, k_all.reshape(kv_shape), v_all.reshape(kv_shape), jnp.stack(hs), jnp.stack(convs)


def kernel(x_prompt, x_sample, cache_k, cache_v, state_h, state_conv, page_table, c_prompt, c_sample, rel_bias, ada_w, ada_b, ln_ffn1, ffn1_gu, ffn1_down, ln_mix, w_in, b_gate, lambda_q1, lambda_k1, lambda_q2, lambda_k2, attn_sub_g, conv_w, conv_b, rg_wa, rg_ba, rg_wx, rg_bx, rg_lambda, w_pa, w_pr, w_o, ln_ffn2, ffn2_gu, ffn2_down, ln_final):
    b_p, s_len, d = x_prompt.shape
    b_d, t_dec, _ = x_sample.shape
    depth = w_in.shape[0]
    per = V7X_MXU_DIM // (rg_wa.shape[-1])
    p = dict(
        rel_bias=rel_bias, ln_ffn1=ln_ffn1, ln_mix=ln_mix, ln_ffn2=ln_ffn2, ln_final=ln_final, b_gate=b_gate,
        attn_sub_g=attn_sub_g, conv_w=conv_w, conv_b=conv_b, rg_ba=rg_ba, rg_bx=rg_bx, rg_lambda=rg_lambda,
        ffn1_gu=ffn1_gu.astype(BF16), ffn1_down=ffn1_down.astype(BF16), w_in=w_in.astype(BF16),
        w_pa=w_pa.astype(BF16), w_pr=w_pr.astype(BF16), w_o=w_o.astype(BF16),
        ffn2_gu=ffn2_gu.astype(BF16), ffn2_down=ffn2_down.astype(BF16),
        rg_wa=jnp.stack([_block_diag(rg_wa[l].astype(BF16), per) for l in range(depth)]),
        rg_wx=jnp.stack([_block_diag(rg_wx[l].astype(BF16), per) for l in range(depth)]),
        lamp=jnp.stack([lambda_q1, lambda_k1, lambda_q2, lambda_k2], axis=1).astype(F32),
        prompt_bias=_prompt_bias(rel_bias, min(ATTN_TILE, s_len), min(ATTN_TILE, s_len)),
        decode_bias=_decode_bias(rel_bias, t_dec, cache_k.shape[2]),
    )

    n_seq = b_p + b_d
    pad = (-n_seq) % V7X_SUBLANES
    c_all = jnp.pad(jnp.concatenate([c_prompt, c_sample], axis=0), ((0, pad), (0, 0)))
    mods = _ada_mods(c_all, ada_w, ada_b)

    def split(rows, repeat):
        out = []
        for l in range(depth):
            parts = [rows[l][:, k * d:(k + 1) * d] for k in range(N_MOD)]
            out.append([jnp.repeat(m, repeat, axis=0) if repeat > 1 else m for m in parts])
        return out

    assert b_p == 1
    mods_p = split(mods[:, :b_p], 1)
    mods_d = split(mods[:, b_p:n_seq], t_dec)

    y_p, k_p, v_p, h_p, conv_p = _run_trunk(x_prompt.reshape(b_p * s_len, d), mods_p, p, batch=b_p)
    y_d, k_d, v_d, h_d, conv_d = _run_trunk(x_sample.reshape(b_d * t_dec, d), mods_d, p, batch=b_d,
                                            cache=(cache_k, cache_v, page_table, state_h, state_conv))
    return (y_p.reshape(b_p, s_len, d), y_d.reshape(b_d, t_dec, d), k_p, v_p, h_p, conv_p, k_d, v_d, h_d, conv_d)
```

```python
import functools
import math

import jax
import jax.numpy as jnp
import numpy as np
from jax import lax
from jax.experimental import pallas as pl
from jax.experimental.pallas import tpu as pltpu

F32 = jnp.float32
BF16 = jnp.bfloat16

N_HEADS = 8
N_KV_HEADS = 4
GROUP = N_HEADS // N_KV_HEADS
HEAD_DIM = 64
V_DIM = 2 * HEAD_DIM
LRU_BLOCKS = 16
CONV_WIDTH = 4
LRU_C = 8.0
N_BUCKETS = 32
MAX_DISTANCE = 128
NORM_EPS = 1e-6
N_MOD = 9

V7X_LANES = 128
V7X_SUBLANES = 8
V7X_MXU_DIM = 256
V7X_VMEM_BYTES = 64 * 1024 * 1024
MIB = 1024 * 1024

ATTN_TILE = V7X_MXU_DIM


def _params(semantics, vmem_mib):
    assert vmem_mib * MIB < V7X_VMEM_BYTES
    return pltpu.CompilerParams(dimension_semantics=semantics, vmem_limit_bytes=vmem_mib * MIB)


def _resident(shape, index_map):
    return pl.BlockSpec(shape, index_map, pipeline_mode=pl.Buffered(1))


def _sigmoid(x):
    return 1.0 / (1.0 + jnp.exp(-x))


def _modulated_norm(x, ln, sc, sh):
    y = x * lax.rsqrt(jnp.mean(x * x, axis=-1, keepdims=True) + NORM_EPS) * ln
    return y * (1.0 + sc) + sh


def _mod_spec(mod, tm):
    if mod.shape[0] == 1:
        return pl.BlockSpec((1, mod.shape[1]), lambda i: (0, 0))
    return pl.BlockSpec((tm, mod.shape[1]), lambda i: (i, 0))


def _ada_kernel(c_ref, w_ref, b_ref, o_ref):
    c = c_ref[...]
    act = c * _sigmoid(c)
    o_ref[...] = jnp.dot(act, w_ref[...], preferred_element_type=F32) + b_ref[...]


def _ada_mods(c_all, ada_w, ada_b):
    depth, d, n = ada_w.shape
    mp = c_all.shape[0]
    tn = n // 8
    assert tn % V7X_LANES == 0 and mp % V7X_SUBLANES == 0
    return pl.pallas_call(
        _ada_kernel,
        grid=(depth, n // tn),
        in_specs=[
            pl.BlockSpec((mp, d), lambda l, j: (0, 0)),
            pl.BlockSpec((None, d, tn), lambda l, j: (l, 0, j)),
            pl.BlockSpec((None, 1, tn), lambda l, j: (l, 0, j)),
        ],
        out_specs=pl.BlockSpec((None, mp, tn), lambda l, j: (l, 0, j)),
        out_shape=jax.ShapeDtypeStruct((depth, mp, n), F32),
        compiler_params=_params(("arbitrary", "arbitrary"), 24),
    )(c_all, ada_w, ada_b.reshape(depth, 1, n))


def _ffn_kernel(x_ref, sh_ref, sc_ref, g_ref, ln_ref, lnf_ref, wgu_ref, wd_ref, o_ref, *, d_ff, tf, final_norm):
    x = x_ref[...]
    h = _modulated_norm(x, ln_ref[...], sc_ref[...], sh_ref[...]).astype(BF16)
    acc = jnp.zeros(x.shape, F32)
    for c in range(d_ff // tf):
        gate = jnp.dot(h, wgu_ref[:, c * tf:(c + 1) * tf], preferred_element_type=F32)
        up = jnp.dot(h, wgu_ref[:, d_ff + c * tf:d_ff + (c + 1) * tf], preferred_element_type=F32)
        act = (gate * _sigmoid(gate) * up).astype(BF16)
        acc = acc + jnp.dot(act, wd_ref[c * tf:(c + 1) * tf, :], preferred_element_type=F32)
    y = x + 0.5 * g_ref[...] * acc
    if final_norm:
        y = y * lax.rsqrt(jnp.mean(y * y, axis=-1, keepdims=True) + NORM_EPS) * lnf_ref[...]
    o_ref[...] = y


def _ffn(x, sh, sc, g, ln, lnf, w_gu, w_down, *, final_norm):
    m, d = x.shape
    d_ff = w_down.shape[0]
    tm = min(512, m)
    tf = V7X_MXU_DIM
    assert m % tm == 0 and d_ff % tf == 0
    row = pl.BlockSpec((tm, d), lambda i: (i, 0))
    vec = pl.BlockSpec((1, d), lambda i: (0, 0))
    return pl.pallas_call(
        functools.partial(_ffn_kernel, d_ff=d_ff, tf=tf, final_norm=final_norm),
        grid=(m // tm,),
        in_specs=[row, _mod_spec(sh, tm), _mod_spec(sc, tm), _mod_spec(g, tm), vec, vec,
                  _resident(w_gu.shape, lambda i: (0, 0)), _resident(w_down.shape, lambda i: (0, 0))],
        out_specs=row,
        out_shape=jax.ShapeDtypeStruct((m, d), F32),
        compiler_params=_params(("arbitrary",), 48),
    )(x, sh, sc, g, ln, lnf, w_gu, w_down)


def _mixin_kernel(x_ref, sh_ref, sc_ref, ln_ref, w_ref, *rest, prompt, n_alias):
    out_refs = rest[n_alias:]
    h = _modulated_norm(x_ref[...], ln_ref[...], sc_ref[...], sh_ref[...]).astype(BF16)
    if prompt:
        qt_ref, k_ref, v_ref, kb_ref, vt_ref, xr_ref, yr_ref, gl_ref = out_refs
    else:
        q_ref, k_ref, v_ref, xr_ref, yr_ref, gl_ref = out_refs
    q_cols, kv_cols = N_HEADS * 2 * HEAD_DIM, N_KV_HEADS * V_DIM
    tm = x_ref.shape[0]

    def proj(col, width):
        return jnp.dot(h, w_ref[:, col:col + width], preferred_element_type=F32)

    q = proj(0, q_cols) * HEAD_DIM ** -0.5
    k = proj(q_cols, kv_cols)
    v = proj(q_cols + kv_cols, kv_cols)
    for hh in range(N_KV_HEADS):
        k_ref[pl.ds(hh, tm, stride=N_KV_HEADS), :] = k[:, hh * V_DIM:(hh + 1) * V_DIM]
        v_ref[pl.ds(hh, tm, stride=N_KV_HEADS), :] = v[:, hh * V_DIM:(hh + 1) * V_DIM]
    if prompt:
        qt_ref[...] = q.T.astype(BF16)
        kb_ref[...] = k.astype(BF16)
        pad_row = lax.broadcasted_iota(jnp.int32, (ACC_ROWS - V_DIM, v.shape[0]), 0)
        ones = jnp.where(pad_row == 0, 1.0, 0.0).astype(BF16)
        for hh in range(N_KV_HEADS):
            vt_ref[hh, :V_DIM, :] = v[:, hh * V_DIM:(hh + 1) * V_DIM].T.astype(BF16)
            vt_ref[hh, V_DIM:, :] = ones
    else:
        q_ref[...] = q.astype(BF16)
    col = q_cols + 2 * kv_cols
    for ref in (xr_ref, yr_ref, gl_ref):
        ref[...] = proj(col, ref.shape[1])
        col += ref.shape[1]


def _mixin(x, sh, sc, ln, w_in, *, prompt, layer, depth, kv_prev):
    m, d = x.shape
    q_cols, kv_cols = N_HEADS * 2 * HEAD_DIM, N_KV_HEADS * V_DIM
    tm = min(ATTN_TILE, m)
    assert m % tm == 0 and w_in.shape[1] == q_cols + 2 * kv_cols + 4 * d

    def rows(width, dtype):
        return pl.BlockSpec((tm, width), lambda i: (i, 0)), jax.ShapeDtypeStruct((m, width), dtype)

    kv_out = (pl.BlockSpec((None, tm * N_KV_HEADS, V_DIM), lambda i: (layer, i, 0)),
              jax.ShapeDtypeStruct((depth, m * N_KV_HEADS, V_DIM), F32))
    if prompt:
        outs = [(pl.BlockSpec((q_cols, tm), lambda i: (0, i)), jax.ShapeDtypeStruct((q_cols, m), BF16)),
                kv_out, kv_out, rows(kv_cols, BF16),
                (pl.BlockSpec((N_KV_HEADS, None, ACC_ROWS, tm), lambda i: (0, i, 0, 0)),
                 jax.ShapeDtypeStruct((N_KV_HEADS, m // tm, ACC_ROWS, tm), BF16))]
    else:
        outs = [rows(q_cols, BF16), kv_out, kv_out]
    outs += [rows(d, F32), rows(d, F32), rows(2 * d, F32)]
    in_specs = [pl.BlockSpec((tm, d), lambda i: (i, 0)), _mod_spec(sh, tm), _mod_spec(sc, tm),
                pl.BlockSpec((1, d), lambda i: (0, 0)), _resident(w_in.shape, lambda i: (0, 0))]
    aliases = {} if kv_prev is None else {len(in_specs): 1, len(in_specs) + 1: 2}
    extra = () if kv_prev is None else tuple(kv_prev)
    in_specs += [pl.BlockSpec(memory_space=pl.ANY)] * len(extra)
    return pl.pallas_call(
        functools.partial(_mixin_kernel, prompt=prompt, n_alias=len(extra)),
        grid=(m // tm,),
        in_specs=in_specs,
        out_specs=[o[0] for o in outs],
        out_shape=[o[1] for o in outs],
        input_output_aliases=aliases,
        compiler_params=_params(("arbitrary",), 48),
    )(x, sh, sc, ln, w_in, *extra)


def _rel_bucket(n):
    max_exact = N_BUCKETS // 2
    nf = jnp.maximum(n, 1).astype(F32)
    large = max_exact + (jnp.log(nf / max_exact) / math.log(MAX_DISTANCE / max_exact)
                         * (N_BUCKETS - max_exact)).astype(jnp.int32)
    large = jnp.minimum(large, N_BUCKETS - 1)
    return jnp.where(n < max_exact, n, large)


def _bias_of_distance(rel_bias, n):
    table = rel_bias.astype(F32) - rel_bias[N_BUCKETS - 1].astype(F32)
    onehot = (_rel_bucket(jnp.maximum(n, 0))[..., None] == jnp.arange(N_BUCKETS, dtype=jnp.int32)).astype(F32)
    b = jnp.einsum('...b,bh->h...', onehot, table, precision=lax.Precision.HIGHEST)
    return jnp.where(n >= 0, b, -jnp.inf)


def _lambda(lamp_ref, lam_init):
    lp = lamp_ref[...]
    return jnp.exp(jnp.sum(lp[0:1] * lp[1:2])) - jnp.exp(jnp.sum(lp[2:3] * lp[3:4])) + lam_init


M_INIT = -1e30


ACC_ROWS = V_DIM + 16
LONG_RUN = 16
SHORT_RUN = 4


def _col_max(s):
    rows = s.shape[0]
    while rows > V7X_SUBLANES:
        rows //= 2
        s = jnp.maximum(s[:rows], s[rows:])
    return jnp.max(s, axis=0, keepdims=True)


def _prompt_attn_kernel(qt_ref, k_ref, vt_ref, bias_ref, lamp_ref, subg_ref, o_ref,
                        qz_scr, sa_scr, sb_scr, m_scr, acc_scr, *, tq, tk, lam_init):
    qi = pl.program_id(1)
    ncomb = 2 * GROUP
    ratio = tq // tk
    row = lax.broadcasted_iota(jnp.int32, (V_DIM, tq), 0)
    for g in range(GROUP):
        qg = qt_ref[g * V_DIM:(g + 1) * V_DIM, :]
        qz_scr[:, (2 * g) * tq:(2 * g + 1) * tq] = jnp.where(row < HEAD_DIM, qg, jnp.zeros_like(qg))
        qz_scr[:, (2 * g + 1) * tq:(2 * g + 2) * tq] = jnp.where(row >= HEAD_DIM, qg, jnp.zeros_like(qg))
    m_scr[...] = jnp.full(m_scr.shape, M_INIT, F32)
    acc_scr[...] = jnp.zeros(acc_scr.shape, F32)
    s_bufs = (sa_scr, sb_scr)

    def logits(j, dst):
        kb = k_ref[pl.ds(pl.multiple_of(j * tk, tk), tk), :]
        dst[...] = jnp.dot(kb, qz_scr[...], preferred_element_type=F32)

    def update(j, src, bias_row):
        vt = vt_ref[j]
        for c in range(ncomb):
            s = src[:, c * tq:(c + 1) * tq]
            if bias_row is not None:
                g = c // 2
                s = s + bias_ref[bias_row:bias_row + tk, g * tq:(g + 1) * tq]
            m_old = m_scr[c]
            m_new = jnp.maximum(m_old, _col_max(s))
            alpha = jnp.exp(m_old - m_new)
            p = jnp.exp(s - m_new).astype(BF16)
            acc_scr[c] = alpha * acc_scr[c] + jnp.dot(vt, p, preferred_element_type=F32)
            m_scr[c] = m_new

    def run(j, count):
        for u in range(count):
            logits(j + u + 1, s_bufs[(u + 1) % 2])
            update(j + u, s_bufs[u % 2], None)

    def by_parity(j, fn):
        for parity in range(2):
            @pl.when(j % 2 == parity)
            def _():
                fn(s_bufs[parity], s_bufs[1 - parity])

    logits(0, sa_scr)

    first = ratio * qi - 1
    n_far = jnp.maximum(first, 0)
    n_long = n_far // LONG_RUN

    def long_runs(t, carry):
        run(t * LONG_RUN, LONG_RUN)
        return carry

    lax.fori_loop(0, n_long, long_runs, 0)
    done = n_long * LONG_RUN
    n_short = (n_far - done) // SHORT_RUN

    def short_runs(t, carry):
        run(done + t * SHORT_RUN, SHORT_RUN)
        return carry

    lax.fori_loop(0, n_short, short_runs, 0)
    done = done + n_short * SHORT_RUN

    def single(j, carry):
        def body(cur, nxt):
            logits(j + 1, nxt)
            update(j, cur, None)
        by_parity(j, body)
        return carry

    lax.fori_loop(done, n_far, single, 0)

    def tail_with_near(cur, nxt):
        bufs = (cur, nxt)
        for t in range(ratio + 1):
            if t < ratio:
                logits(first + t + 1, bufs[(t + 1) % 2])
            update(first + t, bufs[t % 2], t * tk)

    def tail_without_near(cur, nxt):
        bufs = (cur, nxt)
        for t in range(ratio):
            if t + 1 < ratio:
                logits(t + 1, bufs[(t + 1) % 2])
            update(t, bufs[t % 2], (t + 1) * tk)

    @pl.when(qi > 0)
    def _():
        by_parity(first, tail_with_near)

    @pl.when(qi == 0)
    def _():
        tail_without_near(sa_scr, sb_scr)

    lam = _lambda(lamp_ref, lam_init)
    for g in range(GROUP):
        a0, a1 = acc_scr[2 * g], acc_scr[2 * g + 1]
        o = a0[:V_DIM] * (1.0 / a0[V_DIM:V_DIM + 1]) - lam * (a1[:V_DIM] * (1.0 / a1[V_DIM:V_DIM + 1]))
        o = o * lax.rsqrt(jnp.mean(o * o, axis=0, keepdims=True) + NORM_EPS) * subg_ref[...] * (1.0 - lam_init)
        o_ref[:, g * V_DIM:(g + 1) * V_DIM] = o.T.astype(o_ref.dtype)


def _prompt_bias(rel_bias, tq, tk):
    nb = (tq // tk + 1) * tk
    c = jnp.arange(nb, dtype=jnp.int32)[:, None]
    r = jnp.arange(tq, dtype=jnp.int32)[None, :]
    bias = _bias_of_distance(rel_bias, tk + r - c)
    return bias.reshape(N_KV_HEADS, GROUP, nb, tq).transpose(0, 2, 1, 3).reshape(N_KV_HEADS, nb, GROUP * tq)


def _prompt_attention(qt, kb, vt, bias, lamp, subg, lam_init):
    s_len = qt.shape[1]
    nk, tk = vt.shape[1], vt.shape[3]
    tq = tk
    nq = s_len // tq
    ratio = tq // tk
    nb = bias.shape[1]
    assert s_len == nq * tq == nk * tk and tk >= MAX_DISTANCE and nb == (ratio + 1) * tk
    subg_col = subg.reshape(V_DIM, 1)
    ncomb = 2 * GROUP
    return pl.pallas_call(
        functools.partial(_prompt_attn_kernel, tq=tq, tk=tk, lam_init=lam_init),
        grid=(N_KV_HEADS, nq),
        in_specs=[
            pl.BlockSpec((GROUP * V_DIM, tq), lambda h, i: (h, i)),
            pl.BlockSpec((s_len, V_DIM), lambda h, i: (0, h)),
            pl.BlockSpec((None, nk, ACC_ROWS, tk), lambda h, i: (h, 0, 0, 0)),
            pl.BlockSpec((None, nb, GROUP * tq), lambda h, i: (h, 0, 0)),
            pl.BlockSpec(lamp.shape, lambda h, i: (0, 0)),
            pl.BlockSpec(subg_col.shape, lambda h, i: (0, 0)),
        ],
        out_specs=pl.BlockSpec((tq, GROUP * V_DIM), lambda h, i: (i, h)),
        out_shape=jax.ShapeDtypeStruct((s_len, N_HEADS * V_DIM), BF16),
        scratch_shapes=[pltpu.VMEM((V_DIM, ncomb * tq), BF16),
                        pltpu.VMEM((tk, ncomb * tq), F32), pltpu.VMEM((tk, ncomb * tq), F32),
                        pltpu.VMEM((ncomb, 1, tq), F32), pltpu.VMEM((ncomb, ACC_ROWS, tq), F32)],
        compiler_params=_params(("arbitrary", "arbitrary"), 40),
    )(qt, kb, vt, bias, lamp, subg_col)


def _decode_softmax_step(s, v_heads, m_ref, l_ref, acc_ref):
    m_old = m_ref[...]
    m_new = jnp.maximum(m_old, jnp.max(s, axis=-1, keepdims=True))
    alpha = jnp.exp(m_old - m_new)
    p = jnp.exp(s - m_new)
    l_ref[...] = alpha * l_ref[...] + jnp.sum(p, axis=-1, keepdims=True)
    pb = p.astype(BF16)
    nk = pb.shape[1] // len(v_heads)
    rows = pb.shape[0] // N_KV_HEADS
    parts = []
    for h in range(N_KV_HEADS):
        pv = None
        for i, vh in enumerate(v_heads):
            part = jnp.dot(pb[h * rows:(h + 1) * rows, i * nk:(i + 1) * nk], vh(h), preferred_element_type=F32)
            pv = part if pv is None else pv + part
        parts.append(pv)
    acc_ref[...] = alpha * acc_ref[...] + jnp.concatenate(parts, axis=0)
    m_ref[...] = m_new


def _decode_attn_kernel(pt_ref, q_ref, bias_ref, knew_ref, vnew_ref, bnew_ref, lamp_ref, subg_ref, *rest,
                        pages, page, lam_init):
    del pt_ref
    k_refs, v_refs = rest[:pages], rest[pages:2 * pages]
    o_ref, m_scr, l_scr, acc_scr = rest[2 * pages:]
    step = pl.program_id(1)

    @pl.when(step == 0)
    def _():
        m_scr[...] = jnp.full(m_scr.shape, M_INIT, F32)
        l_scr[...] = jnp.zeros(l_scr.shape, F32)
        acc_scr[...] = jnp.zeros(acc_scr.shape, F32)

    q = q_ref[...]
    rows = q.shape[0] // N_KV_HEADS
    nt = (((1,), (1,)), ((), ()))

    def head(ref, h):
        return ref[pl.ds(h, page, stride=N_KV_HEADS), :].astype(BF16)

    def logits(ref):
        return jnp.concatenate(
            [lax.dot_general(q[h * rows:(h + 1) * rows], head(ref, h), nt, preferred_element_type=F32)
             for h in range(N_KV_HEADS)], axis=0)

    s = jnp.concatenate([logits(k_refs[i]) for i in range(pages)], axis=1) + bias_ref[...]
    _decode_softmax_step(s, [functools.partial(head, v_refs[i]) for i in range(pages)], m_scr, l_scr, acc_scr)

    @pl.when(step == pl.num_programs(1) - 1)
    def _():
        s_new = logits(knew_ref) + bnew_ref[...]
        _decode_softmax_step(s_new, [functools.partial(head, vnew_ref)], m_scr, l_scr, acc_scr)
        lam = _lambda(lamp_ref, lam_init)
        out = acc_scr[...] / l_scr[...]
        for h in range(N_KV_HEADS):
            blk = out[h * rows:(h + 1) * rows]
            o = blk[:rows // 2] - lam * blk[rows // 2:]
            o = o * lax.rsqrt(jnp.mean(o * o, axis=-1, keepdims=True) + NORM_EPS) * subg_ref[...] * (1.0 - lam_init)
            o_ref[h * (rows // 2):(h + 1) * (rows // 2), :] = o.astype(o_ref.dtype)


DECODE_PAGES = 16


def _decode_bias(rel_bias, t_len, page):
    rows = N_KV_HEADS * 2 * GROUP * t_len
    pages = DECODE_PAGES
    row_h = (jnp.arange(N_KV_HEADS)[:, None, None, None] * GROUP + jnp.arange(GROUP)[None, None, :, None]
             + jnp.zeros((1, 2, 1, t_len), jnp.int32)).reshape(rows)
    row_t = (jnp.zeros((N_KV_HEADS, 2, GROUP, 1), jnp.int32) + jnp.arange(t_len, dtype=jnp.int32)).reshape(rows)
    key = jnp.arange(page, dtype=jnp.int32)[None, :]
    sel = jax.nn.one_hot(row_h, N_HEADS, dtype=F32)
    last = jnp.sum(jnp.where(sel.T[:, :, None] > 0, _bias_of_distance(rel_bias, page + row_t[:, None] - key), 0.0),
                   axis=0)
    bias_steps = jnp.zeros((2, rows, pages * page), F32).at[1, :, (pages - 1) * page:].set(last)
    n_new = jnp.where(key < t_len, row_t[:, None] - key, -1)
    b_all = _bias_of_distance(rel_bias, n_new)
    b_new = jnp.sum(jnp.where(sel.T[:, :, None] > 0, b_all, 0.0), axis=0)
    return bias_steps, b_new


def _decode_attention(q, k_new, v_new, cache_k, cache_v, layer, page_table, bias_steps, b_new, lamp, subg, lam_init):
    batch, n_pages = page_table.shape
    depth, n_pool, page, kvh, kdim = cache_k.shape
    t_len = q.shape[0] // batch
    rows = N_KV_HEADS * 2 * GROUP * t_len
    pages = DECODE_PAGES
    assert n_pages % pages == 0 and page >= MAX_DISTANCE and t_len <= page
    assert kvh == N_KV_HEADS and kdim == V_DIM
    n_steps = n_pages // pages
    prow = page * kvh
    ck = cache_k.reshape(depth, n_pool, prow, kdim)
    cv = cache_v.reshape(depth, n_pool, prow, kdim)

    q6 = q.reshape(batch, t_len, N_KV_HEADS, GROUP, 2, HEAD_DIM).transpose(0, 2, 4, 3, 1, 5)
    eye_m = jnp.eye(2, dtype=q.dtype)[None, None, :, None, None, :, None]
    q_all = (q6[:, :, :, :, :, None, :] * eye_m).reshape(batch, rows, 2 * HEAD_DIM)

    pad = ((0, 0), (0, (page - t_len) * kvh), (0, 0))
    k_pad = jnp.pad(k_new.reshape(batch, t_len * kvh, kdim), pad)
    v_pad = jnp.pad(v_new.reshape(batch, t_len * kvh, kdim), pad)

    def page_spec(i):
        return pl.BlockSpec((None, None, prow, kdim), lambda b, s, pt: (layer, pt[b, s * pages + i], 0, 0))

    grid_spec = pltpu.PrefetchScalarGridSpec(
        num_scalar_prefetch=1,
        grid=(batch, n_steps),
        in_specs=[
            pl.BlockSpec((None, rows, 2 * HEAD_DIM), lambda b, s, pt: (b, 0, 0)),
            pl.BlockSpec((None, rows, pages * page), lambda b, s, pt: (jnp.where(s == n_steps - 1, 1, 0), 0, 0)),
            pl.BlockSpec((None, prow, kdim), lambda b, s, pt: (b, 0, 0)),
            pl.BlockSpec((None, prow, kdim), lambda b, s, pt: (b, 0, 0)),
            pl.BlockSpec((rows, page), lambda b, s, pt: (0, 0)),
            pl.BlockSpec(lamp.shape, lambda b, s, pt: (0, 0)),
            pl.BlockSpec(subg.shape, lambda b, s, pt: (0, 0)),
        ] + [page_spec(i) for i in range(pages)] * 2,
        out_specs=pl.BlockSpec((None, rows // 2, V_DIM), lambda b, s, pt: (b, 0, 0)),
        scratch_shapes=[pltpu.VMEM((rows, 1), F32), pltpu.VMEM((rows, 1), F32), pltpu.VMEM((rows, V_DIM), F32)],
    )
    out = pl.pallas_call(
        functools.partial(_decode_attn_kernel, pages=pages, page=page, lam_init=lam_init),
        grid_spec=grid_spec,
        out_shape=jax.ShapeDtypeStruct((batch, rows // 2, V_DIM), BF16),
        compiler_params=_params(("arbitrary", "arbitrary"), 32),
    )(page_table, q_all, bias_steps, k_pad, v_pad, b_new, lamp, subg, *([ck] * pages), *([cv] * pages))
    out = out.reshape(batch, N_KV_HEADS, GROUP, t_len, V_DIM).transpose(0, 3, 1, 2, 4)
    return out.reshape(batch * t_len, N_HEADS * V_DIM)


def _gelu_tanh(x):
    return 0.5 * x * (1.0 + jnp.tanh(math.sqrt(2.0 / math.pi) * (x + 0.044715 * (x * x * x))))


def _lru_kernel(xr_ref, yr_ref, h0_ref, buf0_ref, cw_ref, cb_ref, wa_ref, ba_ref, wx_ref, bx_ref, lam_ref,
                o_ref, hlast_ref, tail_ref, xp_scr, h_scr, *, tb):
    pad = V7X_SUBLANES
    tail = CONV_WIDTH - 1

    @pl.when(pl.program_id(1) == 0)
    def _():
        h_scr[...] = h0_ref[...]
        xp_scr[pad - tail:pad, :] = buf0_ref[...]

    x = xr_ref[...]
    xp_scr[pad:pad + tb, :] = x
    cw = cw_ref[...]
    xc = cb_ref[...] + xp_scr[pad - tail:pad - tail + tb, :] * cw[0:1]
    for j in range(1, CONV_WIDTH):
        xc = xc + xp_scr[pad - tail + j:pad - tail + j + tb, :] * cw[j:j + 1]
    new_tail = xp_scr[pad + tb - tail:pad + tb, :]
    tail_ref[...] = new_tail
    xp_scr[pad - tail:pad, :] = new_tail

    xcb = xc.astype(BF16)
    nblk, bw = wa_ref.shape[0], wa_ref.shape[1]

    def gate(w_ref, b_ref):
        parts = [jnp.dot(xcb[:, c * bw:(c + 1) * bw], w_ref[c], preferred_element_type=F32) for c in range(nblk)]
        return _sigmoid(jnp.concatenate(parts, axis=1) + b_ref[...])

    r = gate(wa_ref, ba_ref)
    i = gate(wx_ref, bx_ref)
    neg_lam = -lam_ref[...]
    softplus = jnp.maximum(neg_lam, 0.0) + jnp.log(1.0 + jnp.exp(-jnp.abs(neg_lam)))
    log_a = -LRU_C * r * softplus
    a = jnp.exp(log_a)
    y = -jnp.tanh(log_a) * (a * a + 1.0)
    u = jnp.where(y > 0.0, y * lax.rsqrt(y), 0.0) * i * xc

    sub = V7X_SUBLANES
    nt = tb // sub
    big_a = a.reshape(nt, sub, a.shape[1])
    big_b = u.reshape(nt, sub, a.shape[1])
    row = lax.broadcasted_iota(jnp.int32, big_a.shape, 1)
    shift = 1
    while shift < sub:
        keep = row >= shift
        a_prev = jnp.where(keep, pltpu.roll(big_a, shift, 1), 1.0)
        b_prev = jnp.where(keep, pltpu.roll(big_b, shift, 1), 0.0)
        big_b = big_a * b_prev + big_b
        big_a = big_a * a_prev
        shift *= 2
    carry = h_scr[...]
    tiles = []
    for t in range(nt):
        h_t = big_a[t] * carry + big_b[t]
        carry = h_t[sub - 1:sub]
        tiles.append(h_t)
    h = jnp.concatenate(tiles, axis=0)
    h_scr[...] = h[tb - 1:tb]
    hlast_ref[...] = h[tb - 1:tb]
    o_ref[...] = (h * _gelu_tanh(yr_ref[...])).astype(o_ref.dtype)


def _lru(xr, yr, h0, buf0, conv_w, conv_b, wa, ba, wx, bx, lam, *, batch):
    m, w = xr.shape
    t_len = m // batch
    tb = min(256, t_len)
    nt = t_len // tb
    assert t_len % tb == 0 and tb % V7X_SUBLANES == 0
    tail = CONV_WIDTH - 1
    xr3, yr3 = xr.reshape(batch, t_len, w), yr.reshape(batch, t_len, w)
    rows = pl.BlockSpec((None, tb, w), lambda b, t: (b, t, 0))
    vec = pl.BlockSpec((1, w), lambda b, t: (0, 0))
    wspec = pl.BlockSpec(wa.shape, lambda b, t: (0, 0, 0))
    out, h_last, conv_tail = pl.pallas_call(
        functools.partial(_lru_kernel, tb=tb),
        grid=(batch, nt),
        in_specs=[rows, rows,
                  pl.BlockSpec((None, 1, w), lambda b, t: (b, 0, 0)),
                  pl.BlockSpec((None, tail, w), lambda b, t: (b, 0, 0)),
                  pl.BlockSpec((CONV_WIDTH, w), lambda b, t: (0, 0)), vec, wspec, vec, wspec, vec, vec],
        out_specs=[rows,
                   pl.BlockSpec((None, 1, w), lambda b, t: (b, 0, 0)),
                   pl.BlockSpec((None, tail, w), lambda b, t: (b, 0, 0))],
        out_shape=[jax.ShapeDtypeStruct((batch, t_len, w), BF16),
                   jax.ShapeDtypeStruct((batch, 1, w), F32),
                   jax.ShapeDtypeStruct((batch, tail, w), F32)],
        scratch_shapes=[pltpu.VMEM((V7X_SUBLANES + tb, w), F32), pltpu.VMEM((1, w), F32)],
        compiler_params=_params(("arbitrary", "arbitrary"), 48),
    )(xr3, yr3, h0.reshape(batch, 1, w), buf0, conv_w, conv_b, wa, ba, wx, bx, lam)
    return out.reshape(m, w), h_last.reshape(batch, w), conv_tail


def _block_diag(w, per):
    n, d, e = w.shape
    wg = w.reshape(n // per, per, d, e)
    eye = jnp.eye(per, dtype=w.dtype)
    return jnp.einsum('gpde,pq->gpdqe', wg, eye).reshape(n // per, per * d, per * e)


def _merge_kernel(x_ref, attn_ref, lru_ref, gla_ref, glr_ref, bga_ref, bgr_ref, g_ref, wpa_ref, wpr_ref, wo_ref,
                  o_ref):
    pa = jnp.dot(attn_ref[...], wpa_ref[...], preferred_element_type=F32)
    pr = jnp.dot(lru_ref[...], wpr_ref[...], preferred_element_type=F32)
    merged = _sigmoid(gla_ref[...] + bga_ref[...]) * pa + _sigmoid(glr_ref[...] + bgr_ref[...]) * pr
    o_ref[...] = x_ref[...] + g_ref[...] * jnp.dot(merged.astype(BF16), wo_ref[...], preferred_element_type=F32)


def _merge(x, attn, lru_out, gl, b_gate, g, w_pa, w_pr, w_o):
    m, d = x.shape
    tm = min(512, m)
    assert m % tm == 0
    row = pl.BlockSpec((tm, d), lambda i: (i, 0))
    wspec = _resident((d, d), lambda i: (0, 0))
    return pl.pallas_call(
        _merge_kernel,
        grid=(m // tm,),
        in_specs=[row, row, row, row, pl.BlockSpec((tm, d), lambda i: (i, 1)),
                  pl.BlockSpec((1, d), lambda i: (0, 0)), pl.BlockSpec((1, d), lambda i: (0, 1)),
                  _mod_spec(g, tm), wspec, wspec, wspec],
        out_specs=row,
        out_shape=jax.ShapeDtypeStruct((m, d), F32),
        compiler_params=_params(("arbitrary",), 48),
    )(x, attn, lru_out, gl, gl, b_gate, b_gate, g, w_pa, w_pr, w_o)


def _run_trunk(x, mods, p, *, batch, cache=None):
    depth = p['w_in'].shape[0]
    d = x.shape[1]
    hs, convs = [], []
    kv_all = None
    mix = functools.partial(_mixin, prompt=cache is None, depth=depth)
    for l in range(depth):
        sh1, sc1, g1, sh2, sc2, g2, sh3, sc3, g3 = mods[l]
        lam_init = 0.8 - 0.6 * math.exp(-0.3 * l)
        lamp = p['lamp'][l]
        subg = p['attn_sub_g'][l][None, :]
        x = _ffn(x, sh1, sc1, g1, p['ln_ffn1'][l][None], p['ln_final'][None], p['ffn1_gu'][l], p['ffn1_down'][l],
                 final_norm=False)
        if cache is None:
            qt, k_all, v_all, kb, vt, xr, yr, gl = mix(x, sh2, sc2, p['ln_mix'][l][None], p['w_in'][l], layer=l,
                                                       kv_prev=kv_all)
            attn = _prompt_attention(qt, kb, vt, p['prompt_bias'], lamp, subg, lam_init)
            h0 = jnp.zeros((batch, d), F32)
            buf0 = jnp.zeros((batch, CONV_WIDTH - 1, d), F32)
        else:
            cache_k, cache_v, page_table, state_h, state_conv = cache
            q, k_all, v_all, xr, yr, gl = mix(x, sh2, sc2, p['ln_mix'][l][None], p['w_in'][l], layer=l,
                                              kv_prev=kv_all)
            attn = _decode_attention(q, k_all[l], v_all[l], cache_k, cache_v, l, page_table, *p['decode_bias'],
                                     lamp, subg, lam_init)
            h0, buf0 = state_h[l], state_conv[l]
        kv_all = (k_all, v_all)
        lru_out, h_last, conv_tail = _lru(xr, yr, h0, buf0, p['conv_w'][l], p['conv_b'][l][None], p['rg_wa'][l],
                                          p['rg_ba'][l][None], p['rg_wx'][l], p['rg_bx'][l][None],
                                          p['rg_lambda'][l][None], batch=batch)
        x = _merge(x, attn, lru_out, gl, p['b_gate'][l][None], g2, p['w_pa'][l], p['w_pr'][l], p['w_o'][l])
        x = _ffn(x, sh3, sc3, g3, p['ln_ffn2'][l][None], p['ln_final'][None], p['ffn2_gu'][l], p['ffn2_down'][l],
                 final_norm=(l == depth - 1))
        hs.append(h_last)
        convs.append(conv_tail)
    kv_shape = (depth, batch, x.shape[0] // batch, N_KV_HEADS, V_DIM)
    return x, k_all.reshape(kv_shape), v_all.reshape(kv_shape), jnp.stack(hs), jnp.stack(convs)


def kernel(x_prompt, x_sample, cache_k, cache_v, state_h, state_conv, page_table, c_prompt, c_sample, rel_bias, ada_w, ada_b, ln_ffn1, ffn1_gu, ffn1_down, ln_mix, w_in, b_gate, lambda_q1, lambda_k1, lambda_q2, lambda_k2, attn_sub_g, conv_w, conv_b, rg_wa, rg_ba, rg_wx, rg_bx, rg_lambda, w_pa, w_pr, w_o, ln_ffn2, ffn2_gu, ffn2_down, ln_final):
    b_p, s_len, d = x_prompt.shape
    b_d, t_dec, _ = x_sample.shape
    depth = w_in.shape[0]
    per = V7X_MXU_DIM // (rg_wa.shape[-1])
    p = dict(
        rel_bias=rel_bias, ln_ffn1=ln_ffn1, ln_mix=ln_mix, ln_ffn2=ln_ffn2, ln_final=ln_final, b_gate=b_gate,
        attn_sub_g=attn_sub_g, conv_w=conv_w, conv_b=conv_b, rg_ba=rg_ba, rg_bx=rg_bx, rg_lambda=rg_lambda,
        ffn1_gu=ffn1_gu.astype(BF16), ffn1_down=ffn1_down.astype(BF16), w_in=w_in.astype(BF16),
        w_pa=w_pa.astype(BF16), w_pr=w_pr.astype(BF16), w_o=w_o.astype(BF16),
        ffn2_gu=ffn2_gu.astype(BF16), ffn2_down=ffn2_down.astype(BF16),
        rg_wa=jnp.stack([_block_diag(rg_wa[l].astype(BF16), per) for l in range(depth)]),
        rg_wx=jnp.stack([_block_diag(rg_wx[l].astype(BF16), per) for l in range(depth)]),
        lamp=jnp.stack([lambda_q1, lambda_k1, lambda_q2, lambda_k2], axis=1).astype(F32),
        prompt_bias=_prompt_bias(rel_bias, min(ATTN_TILE, s_len), min(ATTN_TILE, s_len)),
        decode_bias=_decode_bias(rel_bias, t_dec, cache_k.shape[2]),
    )

    n_seq = b_p + b_d
    pad = (-n_seq) % V7X_SUBLANES
    c_all = jnp.pad(jnp.concatenate([c_prompt, c_sample], axis=0), ((0, pad), (0, 0)))
    mods = _ada_mods(c_all, ada_w, ada_b)

    def split(rows, repeat):
        out = []
        for l in range(depth):
            parts = [rows[l][:, k * d:(k + 1) * d] for k in range(N_MOD)]
            out.append([jnp.repeat(m, repeat, axis=0) if repeat > 1 else m for m in parts])
        return out

    assert b_p == 1
    mods_p = split(mods[:, :b_p], 1)
    mods_d = split(mods[:, b_p:n_seq], t_dec)

    y_p, k_p, v_p, h_p, conv_p = _run_trunk(x_prompt.reshape(b_p * s_len, d), mods_p, p, batch=b_p)
    y_d, k_d, v_d, h_d, conv_d = _run_trunk(x_sample.reshape(b_d * t_dec, d), mods_d, p, batch=b_d,
                                            cache=(cache_k, cache_v, page_table, state_h, state_conv))
    return (y_p.reshape(b_p, s_len, d), y_d.reshape(b_d, t_dec, d), k_p, v_p, h_p, conv_p, k_d, v_d, h_d, conv_d)
```

```python
import functools
import math

import jax
import jax.numpy as jnp
import numpy as np
from jax import lax
from jax.experimental import pallas as pl
from jax.experimental.pallas import tpu as pltpu

F32 = jnp.float32
BF16 = jnp.bfloat16

N_HEADS = 8
N_KV_HEADS = 4
GROUP = N_HEADS // N_KV_HEADS
HEAD_DIM = 64
V_DIM = 2 * HEAD_DIM
LRU_BLOCKS = 16
CONV_WIDTH = 4
LRU_C = 8.0
N_BUCKETS = 32
MAX_DISTANCE = 128
NORM_EPS = 1e-6
N_MOD = 9

V7X_LANES = 128
V7X_SUBLANES = 8
V7X_MXU_DIM = 256
V7X_VMEM_BYTES = 64 * 1024 * 1024
MIB = 1024 * 1024

ATTN_TILE = V7X_MXU_DIM


def _params(semantics, vmem_mib):
    assert vmem_mib * MIB < V7X_VMEM_BYTES
    return pltpu.CompilerParams(dimension_semantics=semantics, vmem_limit_bytes=vmem_mib * MIB)


def _resident(shape, index_map):
    return pl.BlockSpec(shape, index_map, pipeline_mode=pl.Buffered(1))


def _sigmoid(x):
    return 1.0 / (1.0 + jnp.exp(-x))


def _modulated_norm(x, ln, sc, sh):
    y = x * lax.rsqrt(jnp.mean(x * x, axis=-1, keepdims=True) + NORM_EPS) * ln
    return y * (1.0 + sc) + sh


def _mod_spec(mod, tm):
    if mod.shape[0] == 1:
        return pl.BlockSpec((1, mod.shape[1]), lambda i: (0, 0))
    return pl.BlockSpec((tm, mod.shape[1]), lambda i: (i, 0))


def _ada_kernel(c_ref, w_ref, b_ref, o_ref):
    c = c_ref[...]
    act = c * _sigmoid(c)
    o_ref[...] = jnp.dot(act, w_ref[...], preferred_element_type=F32) + b_ref[...]


def _ada_mods(c_all, ada_w, ada_b):
    depth, d, n = ada_w.shape
    mp = c_all.shape[0]
    tn = n // 8
    assert tn % V7X_LANES == 0 and mp % V7X_SUBLANES == 0
    return pl.pallas_call(
        _ada_kernel,
        grid=(depth, n // tn),
        in_specs=[
            pl.BlockSpec((mp, d), lambda l, j: (0, 0)),
            pl.BlockSpec((None, d, tn), lambda l, j: (l, 0, j)),
            pl.BlockSpec((None, 1, tn), lambda l, j: (l, 0, j)),
        ],
        out_specs=pl.BlockSpec((None, mp, tn), lambda l, j: (l, 0, j)),
        out_shape=jax.ShapeDtypeStruct((depth, mp, n), F32),
        compiler_params=_params(("arbitrary", "arbitrary"), 24),
    )(c_all, ada_w, ada_b.reshape(depth, 1, n))


def _ffn_kernel(x_ref, sh_ref, sc_ref, g_ref, ln_ref, lnf_ref, wgu_ref, wd_ref, o_ref, *, d_ff, tf, final_norm):
    x = x_ref[...]
    h = _modulated_norm(x, ln_ref[...], sc_ref[...], sh_ref[...]).astype(BF16)
    acc = jnp.zeros(x.shape, F32)
    for c in range(d_ff // tf):
        gate = jnp.dot(h, wgu_ref[:, c * tf:(c + 1) * tf], preferred_element_type=F32)
        up = jnp.dot(h, wgu_ref[:, d_ff + c * tf:d_ff + (c + 1) * tf], preferred_element_type=F32)
        act = (gate * _sigmoid(gate) * up).astype(BF16)
        acc = acc + jnp.dot(act, wd_ref[c * tf:(c + 1) * tf, :], preferred_element_type=F32)
    y = x + 0.5 * g_ref[...] * acc
    if final_norm:
        y = y * lax.rsqrt(jnp.mean(y * y, axis=-1, keepdims=True) + NORM_EPS) * lnf_ref[...]
    o_ref[...] = y


def _ffn(x, sh, sc, g, ln, lnf, w_gu, w_down, *, final_norm):
    m, d = x.shape
    d_ff = w_down.shape[0]
    tm = min(512, m)
    tf = V7X_MXU_DIM
    assert m % tm == 0 and d_ff % tf == 0
    row = pl.BlockSpec((tm, d), lambda i: (i, 0))
    vec = pl.BlockSpec((1, d), lambda i: (0, 0))
    return pl.pallas_call(
        functools.partial(_ffn_kernel, d_ff=d_ff, tf=tf, final_norm=final_norm),
        grid=(m // tm,),
        in_specs=[row, _mod_spec(sh, tm), _mod_spec(sc, tm), _mod_spec(g, tm), vec, vec,
                  _resident(w_gu.shape, lambda i: (0, 0)), _resident(w_down.shape, lambda i: (0, 0))],
        out_specs=row,
        out_shape=jax.ShapeDtypeStruct((m, d), F32),
        compiler_params=_params(("arbitrary",), 48),
    )(x, sh, sc, g, ln, lnf, w_gu, w_down)


def _mixin_kernel(x_ref, sh_ref, sc_ref, ln_ref, w_ref, *rest, prompt, n_alias):
    out_refs = rest[n_alias:]
    h = _modulated_norm(x_ref[...], ln_ref[...], sc_ref[...], sh_ref[...]).astype(BF16)
    if prompt:
        qt_ref, k_ref, v_ref, kb_ref, vt_ref, xr_ref, yr_ref, gl_ref = out_refs
    else:
        q_ref, k_ref, v_ref, xr_ref, yr_ref, gl_ref = out_refs
    q_cols, kv_cols = N_HEADS * 2 * HEAD_DIM, N_KV_HEADS * V_DIM
    tm = x_ref.shape[0]

    def proj(col, width):
        return jnp.dot(h, w_ref[:, col:col + width], preferred_element_type=F32)

    q = proj(0, q_cols) * HEAD_DIM ** -0.5
    k = proj(q_cols, kv_cols)
    v = proj(q_cols + kv_cols, kv_cols)
    for hh in range(N_KV_HEADS):
        k_ref[pl.ds(hh, tm, stride=N_KV_HEADS), :] = k[:, hh * V_DIM:(hh + 1) * V_DIM]
        v_ref[pl.ds(hh, tm, stride=N_KV_HEADS), :] = v[:, hh * V_DIM:(hh + 1) * V_DIM]
    if prompt:
        qt_ref[...] = q.T.astype(BF16)
        kb_ref[...] = k.astype(BF16)
        pad_row = lax.broadcasted_iota(jnp.int32, (ACC_ROWS - V_DIM, v.shape[0]), 0)
        ones = jnp.where(pad_row == 0, 1.0, 0.0).astype(BF16)
        for hh in range(N_KV_HEADS):
            vt_ref[hh, :V_DIM, :] = v[:, hh * V_DIM:(hh + 1) * V_DIM].T.astype(BF16)
            vt_ref[hh, V_DIM:, :] = ones
    else:
        q_ref[...] = q.astype(BF16)
    col = q_cols + 2 * kv_cols
    for ref in (xr_ref, yr_ref, gl_ref):
        ref[...] = proj(col, ref.shape[1])
        col += ref.shape[1]


def _mixin(x, sh, sc, ln, w_in, *, prompt, layer, depth, kv_prev):
    m, d = x.shape
    q_cols, kv_cols = N_HEADS * 2 * HEAD_DIM, N_KV_HEADS * V_DIM
    tm = min(ATTN_TILE, m)
    assert m % tm == 0 and w_in.shape[1] == q_cols + 2 * kv_cols + 4 * d

    def rows(width, dtype):
        return pl.BlockSpec((tm, width), lambda i: (i, 0)), jax.ShapeDtypeStruct((m, width), dtype)

    kv_out = (pl.BlockSpec((None, tm * N_KV_HEADS, V_DIM), lambda i: (layer, i, 0)),
              jax.ShapeDtypeStruct((depth, m * N_KV_HEADS, V_DIM), F32))
    if prompt:
        outs = [(pl.BlockSpec((q_cols, tm), lambda i: (0, i)), jax.ShapeDtypeStruct((q_cols, m), BF16)),
                kv_out, kv_out, rows(kv_cols, BF16),
                (pl.BlockSpec((N_KV_HEADS, None, ACC_ROWS, tm), lambda i: (0, i, 0, 0)),
                 jax.ShapeDtypeStruct((N_KV_HEADS, m // tm, ACC_ROWS, tm), BF16))]
    else:
        outs = [rows(q_cols, BF16), kv_out, kv_out]
    outs += [rows(d, F32), rows(d, F32), rows(2 * d, F32)]
    in_specs = [pl.BlockSpec((tm, d), lambda i: (i, 0)), _mod_spec(sh, tm), _mod_spec(sc, tm),
                pl.BlockSpec((1, d), lambda i: (0, 0)), _resident(w_in.shape, lambda i: (0, 0))]
    aliases = {} if kv_prev is None else {len(in_specs): 1, len(in_specs) + 1: 2}
    extra = () if kv_prev is None else tuple(kv_prev)
    in_specs += [pl.BlockSpec(memory_space=pl.ANY)] * len(extra)
    return pl.pallas_call(
        functools.partial(_mixin_kernel, prompt=prompt, n_alias=len(extra)),
        grid=(m // tm,),
        in_specs=in_specs,
        out_specs=[o[0] for o in outs],
        out_shape=[o[1] for o in outs],
        input_output_aliases=aliases,
        compiler_params=_params(("arbitrary",), 48),
    )(x, sh, sc, ln, w_in, *extra)


def _rel_bucket(n):
    max_exact = N_BUCKETS // 2
    nf = jnp.maximum(n, 1).astype(F32)
    large = max_exact + (jnp.log(nf / max_exact) / math.log(MAX_DISTANCE / max_exact)
                         * (N_BUCKETS - max_exact)).astype(jnp.int32)
    large = jnp.minimum(large, N_BUCKETS - 1)
    return jnp.where(n < max_exact, n, large)


def _bias_of_distance(rel_bias, n):
    table = rel_bias.astype(F32) - rel_bias[N_BUCKETS - 1].astype(F32)
    onehot = (_rel_bucket(jnp.maximum(n, 0))[..., None] == jnp.arange(N_BUCKETS, dtype=jnp.int32)).astype(F32)
    b = jnp.einsum('...b,bh->h...', onehot, table, precision=lax.Precision.HIGHEST)
    return jnp.where(n >= 0, b, -jnp.inf)


def _lambda(lamp_ref, lam_init):
    lp = lamp_ref[...]
    return jnp.exp(jnp.sum(lp[0:1] * lp[1:2])) - jnp.exp(jnp.sum(lp[2:3] * lp[3:4])) + lam_init


M_INIT = -1e30


ACC_ROWS = V_DIM + 16
LONG_RUN = 16
SHORT_RUN = 4


def _col_max(s):
    rows = s.shape[0]
    while rows > V7X_SUBLANES:
        rows //= 2
        s = jnp.maximum(s[:rows], s[rows:])
    return jnp.max(s, axis=0, keepdims=True)


def _prompt_attn_kernel(qt_ref, k_ref, vt_ref, bias_ref, lamp_ref, subg_ref, o_ref,
                        qz_scr, sa_scr, sb_scr, m_scr, acc_scr, *, tq, tk, lam_init):
    qi = pl.program_id(1)
    ncomb = 2 * GROUP
    ratio = tq // tk
    row = lax.broadcasted_iota(jnp.int32, (V_DIM, tq), 0)
    for g in range(GROUP):
        qg = qt_ref[g * V_DIM:(g + 1) * V_DIM, :]
        qz_scr[:, (2 * g) * tq:(2 * g + 1) * tq] = jnp.where(row < HEAD_DIM, qg, jnp.zeros_like(qg))
        qz_scr[:, (2 * g + 1) * tq:(2 * g + 2) * tq] = jnp.where(row >= HEAD_DIM, qg, jnp.zeros_like(qg))
    m_scr[...] = jnp.full(m_scr.shape, M_INIT, F32)
    acc_scr[...] = jnp.zeros(acc_scr.shape, F32)
    s_bufs = (sa_scr, sb_scr)

    def logits(j, dst):
        kb = k_ref[pl.ds(pl.multiple_of(j * tk, tk), tk), :]
        dst[...] = jnp.dot(kb, qz_scr[...], preferred_element_type=F32)

    def update(j, src, bias_row):
        vt = vt_ref[j]
        for c in range(ncomb):
            s = src[:, c * tq:(c + 1) * tq]
            if bias_row is not None:
                g = c // 2
                s = s + bias_ref[bias_row:bias_row + tk, g * tq:(g + 1) * tq]
            m_old = m_scr[c]
            m_new = jnp.maximum(m_old, _col_max(s))
            alpha = jnp.exp(m_old - m_new)
            p = jnp.exp(s - m_new).astype(BF16)
            acc_scr[c] = alpha * acc_scr[c] + jnp.dot(vt, p, preferred_element_type=F32)
            m_scr[c] = m_new

    def run(j, count):
        for u in range(count):
            logits(j + u + 1, s_bufs[(u + 1) % 2])
            update(j + u, s_bufs[u % 2], None)

    def by_parity(j, fn):
        for parity in range(2):
            @pl.when(j % 2 == parity)
            def _():
                fn(s_bufs[parity], s_bufs[1 - parity])

    logits(0, sa_scr)

    first = ratio * qi - 1
    n_far = jnp.maximum(first, 0)
    n_long = n_far // LONG_RUN

    def long_runs(t, carry):
        run(t * LONG_RUN, LONG_RUN)
        return carry

    lax.fori_loop(0, n_long, long_runs, 0)
    done = n_long * LONG_RUN
    n_short = (n_far - done) // SHORT_RUN

    def short_runs(t, carry):
        run(done + t * SHORT_RUN, SHORT_RUN)
        return carry

    lax.fori_loop(0, n_short, short_runs, 0)
    done = done + n_short * SHORT_RUN

    def single(j, carry):
        def body(cur, nxt):
            logits(j + 1, nxt)
            update(j, cur, None)
        by_parity(j, body)
        return carry

    lax.fori_loop(done, n_far, single, 0)

    def tail_with_near(cur, nxt):
        bufs = (cur, nxt)
        for t in range(ratio + 1):
            if t < ratio:
                logits(first + t + 1, bufs[(t + 1) % 2])
            update(first + t, bufs[t % 2], t * tk)

    def tail_without_near(cur, nxt):
        bufs = (cur, nxt)
        for t in range(ratio):
            if t + 1 < ratio:
                logits(t + 1, bufs[(t + 1) % 2])
            update(t, bufs[t % 2], (t + 1) * tk)

    @pl.when(qi > 0)
    def _():
        by_parity(first, tail_with_near)

    @pl.when(qi == 0)
    def _():
        tail_without_near(sa_scr, sb_scr)

    lam = _lambda(lamp_ref, lam_init)
    for g in range(GROUP):
        a0, a1 = acc_scr[2 * g], acc_scr[2 * g + 1]
        o = a0[:V_DIM] * (1.0 / a0[V_DIM:V_DIM + 1]) - lam * (a1[:V_DIM] * (1.0 / a1[V_DIM:V_DIM + 1]))
        o = o * lax.rsqrt(jnp.mean(o * o, axis=0, keepdims=True) + NORM_EPS) * subg_ref[...] * (1.0 - lam_init)
        o_ref[:, g * V_DIM:(g + 1) * V_DIM] = o.T.astype(o_ref.dtype)


def _prompt_bias(rel_bias, tq, tk):
    nb = (tq // tk + 1) * tk
    c = jnp.arange(nb, dtype=jnp.int32)[:, None]
    r = jnp.arange(tq, dtype=jnp.int32)[None, :]
    bias = _bias_of_distance(rel_bias, tk + r - c)
    return bias.reshape(N_KV_HEADS, GROUP, nb, tq).transpose(0, 2, 1, 3).reshape(N_KV_HEADS, nb, GROUP * tq)


def _prompt_attention(qt, kb, vt, bias, lamp, subg, lam_init):
    s_len = qt.shape[1]
    nk, tk = vt.shape[1], vt.shape[3]
    tq = tk
    nq = s_len // tq
    ratio = tq // tk
    nb = bias.shape[1]
    assert s_len == nq * tq == nk * tk and tk >= MAX_DISTANCE and nb == (ratio + 1) * tk
    subg_col = subg.reshape(V_DIM, 1)
    ncomb = 2 * GROUP
    return pl.pallas_call(
        functools.partial(_prompt_attn_kernel, tq=tq, tk=tk, lam_init=lam_init),
        grid=(N_KV_HEADS, nq),
        in_specs=[
            pl.BlockSpec((GROUP * V_DIM, tq), lambda h, i: (h, i)),
            pl.BlockSpec((s_len, V_DIM), lambda h, i: (0, h)),
            pl.BlockSpec((None, nk, ACC_ROWS, tk), lambda h, i: (h, 0, 0, 0)),
            pl.BlockSpec((None, nb, GROUP * tq), lambda h, i: (h, 0, 0)),
            pl.BlockSpec(lamp.shape, lambda h, i: (0, 0)),
            pl.BlockSpec(subg_col.shape, lambda h, i: (0, 0)),
        ],
        out_specs=pl.BlockSpec((tq, GROUP * V_DIM), lambda h, i: (i, h)),
        out_shape=jax.ShapeDtypeStruct((s_len, N_HEADS * V_DIM), BF16),
        scratch_shapes=[pltpu.VMEM((V_DIM, ncomb * tq), BF16),
                        pltpu.VMEM((tk, ncomb * tq), F32), pltpu.VMEM((tk, ncomb * tq), F32),
                        pltpu.VMEM((ncomb, 1, tq), F32), pltpu.VMEM((ncomb, ACC_ROWS, tq), F32)],
        compiler_params=_params(("arbitrary", "arbitrary"), 40),
    )(qt, kb, vt, bias, lamp, subg_col)


def _decode_softmax_step(s, v_heads, m_ref, l_ref, acc_ref):
    m_old = m_ref[...]
    m_new = jnp.maximum(m_old, jnp.max(s, axis=-1, keepdims=True))
    alpha = jnp.exp(m_old - m_new)
    p = jnp.exp(s - m_new)
    l_ref[...] = alpha * l_ref[...] + jnp.sum(p, axis=-1, keepdims=True)
    pb = p.astype(BF16)
    nk = pb.shape[1] // len(v_heads)
    rows = pb.shape[0] // N_KV_HEADS
    parts = []
    for h in range(N_KV_HEADS):
        pv = None
        for i, vh in enumerate(v_heads):
            part = jnp.dot(pb[h * rows:(h + 1) * rows, i * nk:(i + 1) * nk], vh(h), preferred_element_type=F32)
            pv = part if pv is None else pv + part
        parts.append(pv)
    acc_ref[...] = alpha * acc_ref[...] + jnp.concatenate(parts, axis=0)
    m_ref[...] = m_new


def _decode_attn_kernel(pt_ref, q_ref, qt_ref, bias_ref, knew_ref, vnew_ref, bnew_ref, lamp_ref, subg_ref, *rest,
                        pages, page, lam_init):
    del pt_ref
    k_refs, v_refs = rest[:pages], rest[pages:2 * pages]
    o_ref, m_scr, l_scr, acc_scr = rest[2 * pages:]
    step = pl.program_id(1)

    @pl.when(step == 0)
    def _():
        m_scr[...] = jnp.full(m_scr.shape, M_INIT, F32)
        l_scr[...] = jnp.zeros(l_scr.shape, F32)
        acc_scr[...] = jnp.zeros(acc_scr.shape, F32)

    q = q_ref[...]
    rows = q.shape[0] // N_KV_HEADS
    nt = (((1,), (1,)), ((), ()))

    def head(ref, h):
        return ref[pl.ds(h, page, stride=N_KV_HEADS), :].astype(BF16)

    def logits(ref):
        st = None
        for h in range(N_KV_HEADS):
            part = jnp.dot(head(ref, h), qt_ref[h], preferred_element_type=F32)
            st = part if st is None else st + part
        return st.T

    s = jnp.concatenate([logits(k_refs[i]) for i in range(pages)], axis=1) + bias_ref[...]
    _decode_softmax_step(s, [functools.partial(head, v_refs[i]) for i in range(pages)], m_scr, l_scr, acc_scr)

    @pl.when(step == pl.num_programs(1) - 1)
    def _():
        s_new = logits(knew_ref) + bnew_ref[...]
        _decode_softmax_step(s_new, [functools.partial(head, vnew_ref)], m_scr, l_scr, acc_scr)
        lam = _lambda(lamp_ref, lam_init)
        out = acc_scr[...] / l_scr[...]
        for h in range(N_KV_HEADS):
            blk = out[h * rows:(h + 1) * rows]
            o = blk[:rows // 2] - lam * blk[rows // 2:]
            o = o * lax.rsqrt(jnp.mean(o * o, axis=-1, keepdims=True) + NORM_EPS) * subg_ref[...] * (1.0 - lam_init)
            o_ref[h * (rows // 2):(h + 1) * (rows // 2), :] = o.astype(o_ref.dtype)


DECODE_PAGES = 16


def _decode_bias(rel_bias, t_len, page):
    rows = N_KV_HEADS * 2 * GROUP * t_len
    pages = DECODE_PAGES
    row_h = (jnp.arange(N_KV_HEADS)[:, None, None, None] * GROUP + jnp.arange(GROUP)[None, None, :, None]
             + jnp.zeros((1, 2, 1, t_len), jnp.int32)).reshape(rows)
    row_t = (jnp.zeros((N_KV_HEADS, 2, GROUP, 1), jnp.int32) + jnp.arange(t_len, dtype=jnp.int32)).reshape(rows)
    key = jnp.arange(page, dtype=jnp.int32)[None, :]
    sel = jax.nn.one_hot(row_h, N_HEADS, dtype=F32)
    last = jnp.sum(jnp.where(sel.T[:, :, None] > 0, _bias_of_distance(rel_bias, page + row_t[:, None] - key), 0.0),
                   axis=0)
    bias_steps = jnp.zeros((2, rows, pages * page), F32).at[1, :, (pages - 1) * page:].set(last)
    n_new = jnp.where(key < t_len, row_t[:, None] - key, -1)
    b_all = _bias_of_distance(rel_bias, n_new)
    b_new = jnp.sum(jnp.where(sel.T[:, :, None] > 0, b_all, 0.0), axis=0)
    return bias_steps, b_new


def _decode_attention(q, k_new, v_new, cache_k, cache_v, layer, page_table, bias_steps, b_new, lamp, subg, lam_init):
    batch, n_pages = page_table.shape
    depth, n_pool, page, kvh, kdim = cache_k.shape
    t_len = q.shape[0] // batch
    rows = N_KV_HEADS * 2 * GROUP * t_len
    pages = DECODE_PAGES
    assert n_pages % pages == 0 and page >= MAX_DISTANCE and t_len <= page
    assert kvh == N_KV_HEADS and kdim == V_DIM
    n_steps = n_pages // pages
    prow = page * kvh
    ck = cache_k.reshape(depth, n_pool, prow, kdim)
    cv = cache_v.reshape(depth, n_pool, prow, kdim)

    q6 = q.reshape(batch, t_len, N_KV_HEADS, GROUP, 2, HEAD_DIM).transpose(0, 2, 4, 3, 1, 5)
    eye_m = jnp.eye(2, dtype=q.dtype)[None, None, :, None, None, :, None]
    q_all = (q6[:, :, :, :, :, None, :] * eye_m).reshape(batch, rows, 2 * HEAD_DIM)
    head_of_row = jnp.arange(rows) // (rows // N_KV_HEADS)
    hmask = (head_of_row[None, :] == jnp.arange(N_KV_HEADS)[:, None]).astype(q.dtype)
    qt_all = jnp.swapaxes(q_all[:, None] * hmask[None, :, :, None], 2, 3)

    pad = ((0, 0), (0, (page - t_len) * kvh), (0, 0))
    k_pad = jnp.pad(k_new.reshape(batch, t_len * kvh, kdim), pad)
    v_pad = jnp.pad(v_new.reshape(batch, t_len * kvh, kdim), pad)

    def page_spec(i):
        return pl.BlockSpec((None, None, prow, kdim), lambda b, s, pt: (layer, pt[b, s * pages + i], 0, 0))

    grid_spec = pltpu.PrefetchScalarGridSpec(
        num_scalar_prefetch=1,
        grid=(batch, n_steps),
        in_specs=[
            pl.BlockSpec((None, rows, 2 * HEAD_DIM), lambda b, s, pt: (b, 0, 0)),
            pl.BlockSpec((None, N_KV_HEADS, 2 * HEAD_DIM, rows), lambda b, s, pt: (b, 0, 0, 0)),
            pl.BlockSpec((None, rows, pages * page), lambda b, s, pt: (jnp.where(s == n_steps - 1, 1, 0), 0, 0)),
            pl.BlockSpec((None, prow, kdim), lambda b, s, pt: (b, 0, 0)),
            pl.BlockSpec((None, prow, kdim), lambda b, s, pt: (b, 0, 0)),
            pl.BlockSpec((rows, page), lambda b, s, pt: (0, 0)),
            pl.BlockSpec(lamp.shape, lambda b, s, pt: (0, 0)),
            pl.BlockSpec(subg.shape, lambda b, s, pt: (0, 0)),
        ] + [page_spec(i) for i in range(pages)] * 2,
        out_specs=pl.BlockSpec((None, rows // 2, V_DIM), lambda b, s, pt: (b, 0, 0)),
        scratch_shapes=[pltpu.VMEM((rows, 1), F32), pltpu.VMEM((rows, 1), F32), pltpu.VMEM((rows, V_DIM), F32)],
    )
    out = pl.pallas_call(
        functools.partial(_decode_attn_kernel, pages=pages, page=page, lam_init=lam_init),
        grid_spec=grid_spec,
        out_shape=jax.ShapeDtypeStruct((batch, rows // 2, V_DIM), BF16),
        compiler_params=_params(("arbitrary", "arbitrary"), 32),
    )(page_table, q_all, qt_all, bias_steps, k_pad, v_pad, b_new, lamp, subg, *([ck] * pages), *([cv] * pages))
    out = out.reshape(batch, N_KV_HEADS, GROUP, t_len, V_DIM).transpose(0, 3, 1, 2, 4)
    return out.reshape(batch * t_len, N_HEADS * V_DIM)


def _gelu_tanh(x):
    return 0.5 * x * (1.0 + jnp.tanh(math.sqrt(2.0 / math.pi) * (x + 0.044715 * (x * x * x))))


def _lru_kernel(xr_ref, yr_ref, h0_ref, buf0_ref, cw_ref, cb_ref, wa_ref, ba_ref, wx_ref, bx_ref, lam_ref,
                o_ref, hlast_ref, tail_ref, xp_scr, h_scr, *, tb):
    pad = V7X_SUBLANES
    tail = CONV_WIDTH - 1

    @pl.when(pl.program_id(1) == 0)
    def _():
        h_scr[...] = h0_ref[...]
        xp_scr[pad - tail:pad, :] = buf0_ref[...]

    x = xr_ref[...]
    xp_scr[pad:pad + tb, :] = x
    cw = cw_ref[...]
    xc = cb_ref[...] + xp_scr[pad - tail:pad - tail + tb, :] * cw[0:1]
    for j in range(1, CONV_WIDTH):
        xc = xc + xp_scr[pad - tail + j:pad - tail + j + tb, :] * cw[j:j + 1]
    new_tail = xp_scr[pad + tb - tail:pad + tb, :]
    tail_ref[...] = new_tail
    xp_scr[pad - tail:pad, :] = new_tail

    xcb = xc.astype(BF16)
    nblk, bw = wa_ref.shape[0], wa_ref.shape[1]

    def gate(w_ref, b_ref):
        parts = [jnp.dot(xcb[:, c * bw:(c + 1) * bw], w_ref[c], preferred_element_type=F32) for c in range(nblk)]
        return _sigmoid(jnp.concatenate(parts, axis=1) + b_ref[...])

    r = gate(wa_ref, ba_ref)
    i = gate(wx_ref, bx_ref)
    neg_lam = -lam_ref[...]
    softplus = jnp.maximum(neg_lam, 0.0) + jnp.log(1.0 + jnp.exp(-jnp.abs(neg_lam)))
    log_a = -LRU_C * r * softplus
    a = jnp.exp(log_a)
    y = -jnp.tanh(log_a) * (a * a + 1.0)
    u = jnp.where(y > 0.0, y * lax.rsqrt(y), 0.0) * i * xc

    sub = V7X_SUBLANES
    nt = tb // sub
    big_a = a.reshape(nt, sub, a.shape[1])
    big_b = u.reshape(nt, sub, a.shape[1])
    row = lax.broadcasted_iota(jnp.int32, big_a.shape, 1)
    shift = 1
    while shift < sub:
        keep = row >= shift
        a_prev = jnp.where(keep, pltpu.roll(big_a, shift, 1), 1.0)
        b_prev = jnp.where(keep, pltpu.roll(big_b, shift, 1), 0.0)
        big_b = big_a * b_prev + big_b
        big_a = big_a * a_prev
        shift *= 2
    carry = h_scr[...]
    tiles = []
    for t in range(nt):
        h_t = big_a[t] * carry + big_b[t]
        carry = h_t[sub - 1:sub]
        tiles.append(h_t)
    h = jnp.concatenate(tiles, axis=0)
    h_scr[...] = h[tb - 1:tb]
    hlast_ref[...] = h[tb - 1:tb]
    o_ref[...] = (h * _gelu_tanh(yr_ref[...])).astype(o_ref.dtype)


def _lru(xr, yr, h0, buf0, conv_w, conv_b, wa, ba, wx, bx, lam, *, batch):
    m, w = xr.shape
    t_len = m // batch
    tb = min(256, t_len)
    nt = t_len // tb
    assert t_len % tb == 0 and tb % V7X_SUBLANES == 0
    tail = CONV_WIDTH - 1
    xr3, yr3 = xr.reshape(batch, t_len, w), yr.reshape(batch, t_len, w)
    rows = pl.BlockSpec((None, tb, w), lambda b, t: (b, t, 0))
    vec = pl.BlockSpec((1, w), lambda b, t: (0, 0))
    wspec = pl.BlockSpec(wa.shape, lambda b, t: (0, 0, 0))
    out, h_last, conv_tail = pl.pallas_call(
        functools.partial(_lru_kernel, tb=tb),
        grid=(batch, nt),
        in_specs=[rows, rows,
                  pl.BlockSpec((None, 1, w), lambda b, t: (b, 0, 0)),
                  pl.BlockSpec((None, tail, w), lambda b, t: (b, 0, 0)),
                  pl.BlockSpec((CONV_WIDTH, w), lambda b, t: (0, 0)), vec, wspec, vec, wspec, vec, vec],
        out_specs=[rows,
                   pl.BlockSpec((None, 1, w), lambda b, t: (b, 0, 0)),
                   pl.BlockSpec((None, tail, w), lambda b, t: (b, 0, 0))],
        out_shape=[jax.ShapeDtypeStruct((batch, t_len, w), BF16),
                   jax.ShapeDtypeStruct((batch, 1, w), F32),
                   jax.ShapeDtypeStruct((batch, tail, w), F32)],
        scratch_shapes=[pltpu.VMEM((V7X_SUBLANES + tb, w), F32), pltpu.VMEM((1, w), F32)],
        compiler_params=_params(("arbitrary", "arbitrary"), 48),
    )(xr3, yr3, h0.reshape(batch, 1, w), buf0, conv_w, conv_b, wa, ba, wx, bx, lam)
    return out.reshape(m, w), h_last.reshape(batch, w), conv_tail


def _block_diag(w, per):
    n, d, e = w.shape
    wg = w.reshape(n // per, per, d, e)
    eye = jnp.eye(per, dtype=w.dtype)
    return jnp.einsum('gpde,pq->gpdqe', wg, eye).reshape(n // per, per * d, per * e)


def _merge_kernel(x_ref, attn_ref, lru_ref, gla_ref, glr_ref, bga_ref, bgr_ref, g_ref, wpa_ref, wpr_ref, wo_ref,
                  o_ref):
    pa = jnp.dot(attn_ref[...], wpa_ref[...], preferred_element_type=F32)
    pr = jnp.dot(lru_ref[...], wpr_ref[...], preferred_element_type=F32)
    merged = _sigmoid(gla_ref[...] + bga_ref[...]) * pa + _sigmoid(glr_ref[...] + bgr_ref[...]) * pr
    o_ref[...] = x_ref[...] + g_ref[...] * jnp.dot(merged.astype(BF16), wo_ref[...], preferred_element_type=F32)


def _merge(x, attn, lru_out, gl, b_gate, g, w_pa, w_pr, w_o):
    m, d = x.shape
    tm = min(512, m)
    assert m % tm == 0
    row = pl.BlockSpec((tm, d), lambda i: (i, 0))
    wspec = _resident((d, d), lambda i: (0, 0))
    return pl.pallas_call(
        _merge_kernel,
        grid=(m // tm,),
        in_specs=[row, row, row, row, pl.BlockSpec((tm, d), lambda i: (i, 1)),
                  pl.BlockSpec((1, d), lambda i: (0, 0)), pl.BlockSpec((1, d), lambda i: (0, 1)),
                  _mod_spec(g, tm), wspec, wspec, wspec],
        out_specs=row,
        out_shape=jax.ShapeDtypeStruct((m, d), F32),
        compiler_params=_params(("arbitrary",), 48),
    )(x, attn, lru_out, gl, gl, b_gate, b_gate, g, w_pa, w_pr, w_o)


def _run_trunk(x, mods, p, *, batch, cache=None):
    depth = p['w_in'].shape[0]
    d = x.shape[1]
    hs, convs = [], []
    kv_all = None
    mix = functools.partial(_mixin, prompt=cache is None, depth=depth)
    for l in range(depth):
        sh1, sc1, g1, sh2, sc2, g2, sh3, sc3, g3 = mods[l]
        lam_init = 0.8 - 0.6 * math.exp(-0.3 * l)
        lamp = p['lamp'][l]
        subg = p['attn_sub_g'][l][None, :]
        x = _ffn(x, sh1, sc1, g1, p['ln_ffn1'][l][None], p['ln_final'][None], p['ffn1_gu'][l], p['ffn1_down'][l],
                 final_norm=False)
        if cache is None:
            qt, k_all, v_all, kb, vt, xr, yr, gl = mix(x, sh2, sc2, p['ln_mix'][l][None], p['w_in'][l], layer=l,
                                                       kv_prev=kv_all)
            attn = _prompt_attention(qt, kb, vt, p['prompt_bias'], lamp, subg, lam_init)
            h0 = jnp.zeros((batch, d), F32)
            buf0 = jnp.zeros((batch, CONV_WIDTH - 1, d), F32)
        else:
            cache_k, cache_v, page_table, state_h, state_conv = cache
            q, k_all, v_all, xr, yr, gl = mix(x, sh2, sc2, p['ln_mix'][l][None], p['w_in'][l], layer=l,
                                              kv_prev=kv_all)
            attn = _decode_attention(q, k_all[l], v_all[l], cache_k, cache_v, l, page_table, *p['decode_bias'],
                                     lamp, subg, lam_init)
            h0, buf0 = state_h[l], state_conv[l]
        kv_all = (k_all, v_all)
        lru_out, h_last, conv_tail = _lru(xr, yr, h0, buf0, p['conv_w'][l], p['conv_b'][l][None], p['rg_wa'][l],
                                          p['rg_ba'][l][None], p['rg_wx'][l], p['rg_bx'][l][None],
                                          p['rg_lambda'][l][None], batch=batch)
        x = _merge(x, attn, lru_out, gl, p['b_gate'][l][None], g2, p['w_pa'][l], p['w_pr'][l], p['w_o'][l])
        x = _ffn(x, sh3, sc3, g3, p['ln_ffn2'][l][None], p['ln_final'][None], p['ffn2_gu'][l], p['ffn2_down'][l],
                 final_norm=(l == depth - 1))
        hs.append(h_last)
        convs.append(conv_tail)
    kv_shape = (depth, batch, x.shape[0] // batch, N_KV_HEADS, V_DIM)
    return x, k_all.reshape(kv_shape), v_all.reshape(kv_shape), jnp.stack(hs), jnp.stack(convs)


def kernel(x_prompt, x_sample, cache_k, cache_v, state_h, state_conv, page_table, c_prompt, c_sample, rel_bias, ada_w, ada_b, ln_ffn1, ffn1_gu, ffn1_down, ln_mix, w_in, b_gate, lambda_q1, lambda_k1, lambda_q2, lambda_k2, attn_sub_g, conv_w, conv_b, rg_wa, rg_ba, rg_wx, rg_bx, rg_lambda, w_pa, w_pr, w_o, ln_ffn2, ffn2_gu, ffn2_down, ln_final):
    b_p, s_len, d = x_prompt.shape
    b_d, t_dec, _ = x_sample.shape
    depth = w_in.shape[0]
    per = V7X_MXU_DIM // (rg_wa.shape[-1])
    p = dict(
        rel_bias=rel_bias, ln_ffn1=ln_ffn1, ln_mix=ln_mix, ln_ffn2=ln_ffn2, ln_final=ln_final, b_gate=b_gate,
        attn_sub_g=attn_sub_g, conv_w=conv_w, conv_b=conv_b, rg_ba=rg_ba, rg_bx=rg_bx, rg_lambda=rg_lambda,
        ffn1_gu=ffn1_gu.astype(BF16), ffn1_down=ffn1_down.astype(BF16), w_in=w_in.astype(BF16),
        w_pa=w_pa.astype(BF16), w_pr=w_pr.astype(BF16), w_o=w_o.astype(BF16),
        ffn2_gu=ffn2_gu.astype(BF16), ffn2_down=ffn2_down.astype(BF16),
        rg_wa=jnp.stack([_block_diag(rg_wa[l].astype(BF16), per) for l in range(depth)]),
        rg_wx=jnp.stack([_block_diag(rg_wx[l].astype(BF16), per) for l in range(depth)]),
        lamp=jnp.stack([lambda_q1, lambda_k1, lambda_q2, lambda_k2], axis=1).astype(F32),
        prompt_bias=_prompt_bias(rel_bias, min(ATTN_TILE, s_len), min(ATTN_TILE, s_len)),
        decode_bias=_decode_bias(rel_bias, t_dec, cache_k.shape[2]),
    )

    n_seq = b_p + b_d
    pad = (-n_seq) % V7X_SUBLANES
    c_all = jnp.pad(jnp.concatenate([c_prompt, c_sample], axis=0), ((0, pad), (0, 0)))
    mods = _ada_mods(c_all, ada_w, ada_b)

    def split(rows, repeat):
        out = []
        for l in range(depth):
            parts = [rows[l][:, k * d:(k + 1) * d] for k in range(N_MOD)]
            out.append([jnp.repeat(m, repeat, axis=0) if repeat > 1 else m for m in parts])
        return out

    assert b_p == 1
    mods_p = split(mods[:, :b_p], 1)
    mods_d = split(mods[:, b_p:n_seq], t_dec)

    y_p, k_p, v_p, h_p, conv_p = _run_trunk(x_prompt.reshape(b_p * s_len, d), mods_p, p, batch=b_p)
    y_d, k_d, v_d, h_d, conv_d = _run_trunk(x_sample.reshape(b_d * t_dec, d), mods_d, p, batch=b_d,
                                            cache=(cache_k, cache_v, page_table, state_h, state_conv))
    return (y_p.reshape(b_p, s_len, d), y_d.reshape(b_d, t_dec, d), k_p, v_p, h_p, conv_p, k_d, v_d, h_d, conv_d)
```
